```python
import math
import jax
import jax.numpy as jnp
from jax import lax
import numpy as np

D_MODEL = 1024
BATCH = 8
SEQ = 8192
DEPTH = 2

N_MIXERS = 4
GROUP_WIDTH = D_MODEL // N_MIXERS
MIX_WIDTH = N_MIXERS * GROUP_WIDTH
N_HEADS = 4
HEAD_DIM = GROUP_WIDTH // N_HEADS
Q_BLOCK = 128
NORM_EPS = 1e-6
DIFF_QK_DIM = HEAD_DIM // 2
ALIBI_MAX_EXP = 8.0
SSM_D_STATE = 128
SSM_BC_GROUPS = 2
SSM_CONV = 4
SSM_CHUNK = 128
SSM_CONV_DIM = GROUP_WIDTH + 2 * SSM_BC_GROUPS * SSM_D_STATE
RWKV_DECAY_RANK = 32
RWKV_A_RANK = 32
RWKV_GATE_RANK = 64
RWKV_PROJ = 3 * GROUP_WIDTH + RWKV_DECAY_RANK + RWKV_A_RANK + RWKV_GATE_RANK
RWKV_LN_EPS = 64e-5
D_FF = 2816
FFN_CONV = 3
IN_SPLITS = (2 * N_HEADS * DIFF_QK_DIM, 2 * N_HEADS * DIFF_QK_DIM, GROUP_WIDTH,
             GROUP_WIDTH, SSM_CONV_DIM, N_HEADS,
             GROUP_WIDTH, GROUP_WIDTH, GROUP_WIDTH, N_HEADS,
             RWKV_PROJ)
D_IN = sum(IN_SPLITS)

kernel_name = 'hymba_style_diff_ssd_fox_rwkv7_trunk'


def rms_norm(x, g, eps=NORM_EPS):
    xf = x.astype(jnp.float32)
    y = xf * lax.rsqrt(jnp.mean(xf * xf, axis=-1, keepdims=True) + eps)
    return (y * g.astype(jnp.float32)).astype(x.dtype)


def _split_cols(x, sizes):
    out, start = [], 0
    for n in sizes:
        out.append(x[..., start:start + n])
        start += n
    return out


def causal_dwconv(x, w, b):
    k = w.shape[0]
    y = lax.conv_general_dilated(x, w[:, None, :].astype(x.dtype), window_strides=(1,),
                                 padding=[(k - 1, 0)], dimension_numbers=('NWC', 'WIO', 'NWC'),
                                 feature_group_count=x.shape[-1])
    return y + b.astype(x.dtype)


def alibi_slopes(n):
    return jnp.exp2(-ALIBI_MAX_EXP * jnp.arange(1, n + 1, dtype=jnp.float32) / n)


def _query_blocks(t):
    b, m, s, d = t.shape
    return jnp.moveaxis(t.reshape(b, m, s // Q_BLOCK, Q_BLOCK, d), 2, 0)


def diff_attention(q, k, v, lam, lam_init, subln_g):
    b, s, _ = q.shape
    nb = s // Q_BLOCK
    q = (q.reshape(b, s, 2 * N_HEADS, DIFF_QK_DIM) * DIFF_QK_DIM ** -0.5).transpose(0, 2, 1, 3)
    k = k.reshape(b, s, 2 * N_HEADS, DIFF_QK_DIM).transpose(0, 2, 1, 3)
    v = v.reshape(b, s, N_HEADS, HEAD_DIM).transpose(0, 2, 1, 3)
    slopes = jnp.repeat(alibi_slopes(N_HEADS), 2)
    pos = jnp.arange(s)

    def block(args):
        q_blk, i = args
        t = i * Q_BLOCK + jnp.arange(Q_BLOCK)
        dist = (t[:, None] - pos[None, :]).astype(jnp.float32)
        logits = (jnp.einsum('bmqd,bmsd->bmqs', q_blk, k).astype(jnp.float32)
                  - slopes[:, None, None] * dist)
        logits = jnp.where(dist >= 0, logits, -jnp.inf)
        p = jax.nn.softmax(logits, axis=-1).reshape(b, N_HEADS, 2, Q_BLOCK, s)
        att = p[:, :, 0] - lam * p[:, :, 1]
        return jnp.einsum('bhqs,bhsd->bhqd', att.astype(v.dtype), v)

    o = lax.map(block, (_query_blocks(q), jnp.arange(nb)))
    o = jnp.moveaxis(o, 0, 2).reshape(b, N_HEADS, s, HEAD_DIM)
    o = rms_norm(o, subln_g) * (1.0 - lam_init)
    return o.transpose(0, 2, 1, 3).reshape(b, s, GROUP_WIDTH)


def forgetting_attention(q, k, v, f_logit, norm_g):
    b, s, _ = q.shape
    nb = s // Q_BLOCK
    q = (q.reshape(b, s, N_HEADS, HEAD_DIM) * HEAD_DIM ** -0.5).transpose(0, 2, 1, 3)
    k = k.reshape(b, s, N_HEADS, HEAD_DIM).transpose(0, 2, 1, 3)
    v = v.reshape(b, s, N_HEADS, HEAD_DIM).transpose(0, 2, 1, 3)
    c = jnp.cumsum(jax.nn.log_sigmoid(f_logit.astype(jnp.float32)), axis=1).transpose(0, 2, 1)
    c_blocks = jnp.moveaxis(c.reshape(b, N_HEADS, nb, Q_BLOCK), 2, 0)
    pos = jnp.arange(s)

    def block(args):
        q_blk, c_blk, i = args
        t = i * Q_BLOCK + jnp.arange(Q_BLOCK)
        logits = (jnp.einsum('bhqd,bhsd->bhqs', q_blk, k).astype(jnp.float32)
                  + c_blk[..., :, None] - c[:, :, None, :])
        logits = jnp.where(pos[None, :] <= t[:, None], logits, -jnp.inf)
        p = jax.nn.softmax(logits, axis=-1)
        return jnp.einsum('bhqs,bhsd->bhqd', p.astype(v.dtype), v)

    o = lax.map(block, (_query_blocks(q), c_blocks, jnp.arange(nb)))
    o = jnp.moveaxis(o, 0, 2).reshape(b, N_HEADS, s, HEAD_DIM)
    o = rms_norm(o, norm_g)
    return o.transpose(0, 2, 1, 3).reshape(b, s, GROUP_WIDTH)


def segsum(a):
    cs = jnp.cumsum(a, axis=-1)
    n = a.shape[-1]
    mask = jnp.tril(jnp.ones((n, n), dtype=bool))
    return jnp.where(mask, cs[..., :, None] - cs[..., None, :], -jnp.inf)


def ssd_chunked(x, dA, bh, ch):
    b, s, h, p = x.shape
    n = bh.shape[-1]
    c, l = s // SSM_CHUNK, SSM_CHUNK
    x = x.reshape(b, c, l, h, p)
    bh = bh.reshape(b, c, l, h, n)
    ch = ch.reshape(b, c, l, h, n)
    dA = dA.reshape(b, c, l, h).transpose(0, 3, 1, 2)
    a_cs = jnp.cumsum(dA, axis=-1)
    decay_in = jnp.exp(segsum(dA))
    scores = jnp.einsum('bclhn,bcshn->bhcls', ch, bh) * decay_in
    y_diag = jnp.einsum('bhcls,bcshp->bclhp', scores, x)
    decay_states = jnp.exp(a_cs[..., -1:] - a_cs)
    states = jnp.einsum('bclhn,bhcl,bclhp->bchpn', bh, decay_states, x)
    states = jnp.concatenate([jnp.zeros_like(states[:, :1]), states], axis=1)
    decay_chunk = jnp.exp(segsum(jnp.pad(a_cs[..., -1], ((0, 0), (0, 0), (1, 0)))))
    start_states = jnp.einsum('bhzc,bchpn->bzhpn', decay_chunk, states)[:, :-1]
    y_off = jnp.einsum('bclhn,bchpn,bhcl->bclhp', ch, start_states, jnp.exp(a_cs))
    return (y_diag + y_off).reshape(b, s, h, p)


def ssd_mixer(z, xbc, dt_raw, conv_w, conv_b, dt_bias, a_log, d_skip, norm_g):
    b, s, _ = z.shape
    xbc = jax.nn.silu(causal_dwconv(xbc, conv_w, conv_b)).astype(jnp.float32)
    xs, bm, cm = _split_cols(xbc, (GROUP_WIDTH, SSM_BC_GROUPS * SSM_D_STATE, SSM_BC_GROUPS * SSM_D_STATE))
    rep = N_HEADS // SSM_BC_GROUPS
    x_h = xs.reshape(b, s, N_HEADS, HEAD_DIM)
    bh = jnp.repeat(bm.reshape(b, s, SSM_BC_GROUPS, SSM_D_STATE), rep, axis=2)
    ch = jnp.repeat(cm.reshape(b, s, SSM_BC_GROUPS, SSM_D_STATE), rep, axis=2)
    dt = jax.nn.softplus(dt_raw.astype(jnp.float32) + dt_bias.astype(jnp.float32))
    a = -jnp.exp(a_log.astype(jnp.float32))
    y = ssd_chunked(x_h * dt[..., None], dt * a, bh, ch)
    y = (y + x_h * d_skip.astype(jnp.float32)[:, None]).reshape(b, s, GROUP_WIDTH)
    yg = (y * jax.nn.silu(z.astype(jnp.float32))).reshape(b, s, SSM_BC_GROUPS, GROUP_WIDTH // SSM_BC_GROUPS)
    yg = yg * lax.rsqrt(jnp.mean(yg * yg, axis=-1, keepdims=True) + NORM_EPS)
    return yg.reshape(b, s, GROUP_WIDTH) * norm_g.astype(jnp.float32)


def wkv7_scan(r, decay, k, v, kk, a):
    b, s, h, n = r.shape

    def step(state, inp):
        r_t, w_t, k_t, v_t, kk_t, a_t = inp
        sa = jnp.einsum('bhvk,bhk->bhv', state, -kk_t)
        state = (state * w_t[:, :, None, :] + sa[..., None] * (kk_t * a_t)[:, :, None, :]
                 + v_t[..., None] * k_t[:, :, None, :])
        return state, jnp.einsum('bhvk,bhk->bhv', state, r_t)

    xs = tuple(jnp.moveaxis(t, 1, 0) for t in (r, decay, k, v, kk, a))
    _, ys = lax.scan(step, jnp.zeros((b, h, n, n), jnp.float32), xs)
    return jnp.moveaxis(ys, 0, 1)


def rwkv7_time_mix(p, mu, w0, w2, a0, a2, g2, k_k, k_a, r_k, ln_w, ln_b):
    b, s, _ = p.shape
    p = p.astype(jnp.float32)
    p_prev = jnp.pad(p, ((0, 0), (1, 0), (0, 0)))[:, :-1]
    p = p + (p_prev - p) * mu
    r, k, v, xw, xa, xg = _split_cols(p, (GROUP_WIDTH, GROUP_WIDTH, GROUP_WIDTH,
                                          RWKV_DECAY_RANK, RWKV_A_RANK, RWKV_GATE_RANK))
    w = -jax.nn.softplus(-(w0 + jnp.tanh(xw) @ w2)) - 0.5
    decay = jnp.exp(-jnp.exp(w))
    a = jax.nn.sigmoid(a0 + xa @ a2)
    g = jax.nn.sigmoid(xg) @ g2
    heads = lambda t: t.reshape(b, s, N_HEADS, HEAD_DIM)
    kk = heads(k * k_k)
    kk = kk * lax.rsqrt(jnp.sum(kk * kk, axis=-1, keepdims=True) + 1e-12)
    k = k * (1.0 + (a - 1.0) * k_a)
    r, k, v, decay, a = heads(r), heads(k), heads(v), heads(decay), heads(a)
    o = wkv7_scan(r, decay, k, v, kk, a)
    mean = jnp.mean(o, axis=-1, keepdims=True)
    var = jnp.mean(jnp.square(o - mean), axis=-1, keepdims=True)
    o = ((o - mean) * lax.rsqrt(var + RWKV_LN_EPS)).reshape(b, s, GROUP_WIDTH) * ln_w + ln_b
    o = o + (jnp.sum(r * k * r_k, axis=-1, keepdims=True) * v).reshape(b, s, GROUP_WIDTH)
    return o * g


def conv_glu_ffn(h, w_gate, w_up, conv_w, conv_b, w_down):
    gate = causal_dwconv(h @ w_gate, conv_w, conv_b)
    return (jax.nn.gelu(gate, approximate=True) * (h @ w_up)) @ w_down


def setup_inputs(seed: int = 0) -> dict:
    key = jax.random.key(seed)
    keys = iter(jax.random.split(key, 48))
    f32 = jnp.float32
    L, G, H = DEPTH, GROUP_WIDTH, N_HEADS

    def normal(shape, scale):
        return jax.random.normal(next(keys), shape, f32) * scale

    def gain(shape):
        return 1.0 + normal(shape, 0.05)

    def uniform(shape, lo, hi):
        return jax.random.uniform(next(keys), shape, f32, lo, hi)

    dt_init = jnp.exp(uniform((L, H), math.log(1e-3), math.log(1e-1)))
    return {
        'x': normal((BATCH, SEQ, D_MODEL), 1.0),
        'norm_mix_pre': gain((L, D_MODEL)),
        'norm_mix_post': gain((L, D_MODEL)),
        'norm_ffn_pre': gain((L, D_MODEL)),
        'norm_ffn_post': gain((L, D_MODEL)),
        'w_in': normal((L, D_MODEL, D_IN), D_MODEL ** -0.5),
        'w_out': normal((L, MIX_WIDTH, D_MODEL), MIX_WIDTH ** -0.5),
        'diff_lambda_q1': normal((L, DIFF_QK_DIM), 0.1),
        'diff_lambda_k1': normal((L, DIFF_QK_DIM), 0.1),
        'diff_lambda_q2': normal((L, DIFF_QK_DIM), 0.1),
        'diff_lambda_k2': normal((L, DIFF_QK_DIM), 0.1),
        'diff_subln': gain((L, HEAD_DIM)),
        'ssm_conv_w': normal((L, SSM_CONV, SSM_CONV_DIM), SSM_CONV ** -0.5),
        'ssm_conv_b': normal((L, SSM_CONV_DIM), 0.02),
        'ssm_dt_bias': dt_init + jnp.log(-jnp.expm1(-dt_init)),
        'ssm_a_log': jnp.log(uniform((L, H), 1.0, 16.0)),
        'ssm_d': gain((L, H)),
        'ssm_norm': gain((L, G)),
        'fox_f_bias': jnp.linspace(1.0, 4.0, H)[None, :] + normal((L, H), 0.1),
        'fox_norm': gain((L, HEAD_DIM)),
        'rwkv_mu': uniform((L, RWKV_PROJ), 0.2, 0.8),
        'rwkv_w0': jnp.linspace(-6.0, -1.0, G)[None, :] + normal((L, G), 0.1),
        'rwkv_w2': normal((L, RWKV_DECAY_RANK, G), 0.5 * RWKV_DECAY_RANK ** -0.5),
        'rwkv_a0': normal((L, G), 0.1),
        'rwkv_a2': normal((L, RWKV_A_RANK, G), 0.5 * RWKV_A_RANK ** -0.5),
        'rwkv_g2': normal((L, RWKV_GATE_RANK, G), RWKV_GATE_RANK ** -0.5),
        'rwkv_k_k': 0.85 + normal((L, G), 0.02),
        'rwkv_k_a': gain((L, G)),
        'rwkv_r_k': normal((L, H, HEAD_DIM), 0.1),
        'rwkv_ln_w': gain((L, G)),
        'rwkv_ln_b': normal((L, G), 0.02),
        'ffn_w_gate': normal((L, D_MODEL, D_FF), D_MODEL ** -0.5),
        'ffn_w_up': normal((L, D_MODEL, D_FF), D_MODEL ** -0.5),
        'ffn_conv_w': normal((L, FFN_CONV, D_FF), FFN_CONV ** -0.5),
        'ffn_conv_b': normal((L, D_FF), 0.02),
        'ffn_w_down': normal((L, D_FF, D_MODEL), D_FF ** -0.5),
    }


def reference(x, norm_mix_pre, norm_mix_post, norm_ffn_pre, norm_ffn_post, w_in, w_out,
              diff_lambda_q1, diff_lambda_k1, diff_lambda_q2, diff_lambda_k2, diff_subln,
              ssm_conv_w, ssm_conv_b, ssm_dt_bias, ssm_a_log, ssm_d, ssm_norm,
              fox_f_bias, fox_norm,
              rwkv_mu, rwkv_w0, rwkv_w2, rwkv_a0, rwkv_a2, rwkv_g2, rwkv_k_k, rwkv_k_a, rwkv_r_k,
              rwkv_ln_w, rwkv_ln_b,
              ffn_w_gate, ffn_w_up, ffn_conv_w, ffn_conv_b, ffn_w_down):
    for l in range(DEPTH):
        h = rms_norm(x, norm_mix_pre[l])
        (dq, dk, dv, sz, sxbc, sdt, fq, fk, fv, ff, rp) = _split_cols(h @ w_in[l], IN_SPLITS)
        lam_init = 0.8 - 0.6 * math.exp(-0.3 * l)
        lam = (jnp.exp(jnp.sum(diff_lambda_q1[l] * diff_lambda_k1[l]).astype(jnp.float32))
               - jnp.exp(jnp.sum(diff_lambda_q2[l] * diff_lambda_k2[l]).astype(jnp.float32)) + lam_init)
        y_diff = diff_attention(dq, dk, dv, lam, lam_init, diff_subln[l])
        y_ssm = ssd_mixer(sz, sxbc, sdt, ssm_conv_w[l], ssm_conv_b[l], ssm_dt_bias[l], ssm_a_log[l],
                          ssm_d[l], ssm_norm[l])
        y_fox = forgetting_attention(fq, fk, fv, ff + fox_f_bias[l], fox_norm[l])
        y_rwkv = rwkv7_time_mix(rp, rwkv_mu[l], rwkv_w0[l], rwkv_w2[l], rwkv_a0[l], rwkv_a2[l],
                                rwkv_g2[l], rwkv_k_k[l], rwkv_k_a[l], rwkv_r_k[l],
                                rwkv_ln_w[l], rwkv_ln_b[l])
        y = jnp.concatenate([t.astype(x.dtype) for t in (y_diff, y_ssm, y_fox, y_rwkv)], axis=-1) @ w_out[l]
        x = x + rms_norm(y, norm_mix_post[l])
        h = rms_norm(x, norm_ffn_pre[l])
        f = conv_glu_ffn(h, ffn_w_gate[l], ffn_w_up[l], ffn_conv_w[l], ffn_conv_b[l], ffn_w_down[l])
        x = x + rms_norm(f, norm_ffn_post[l])
    return x
```

```python
import functools
import math

import jax
import jax.numpy as jnp
from jax import lax
from jax.experimental import pallas as pl
from jax.experimental.pallas import tpu as pltpu

F32 = jnp.float32
BF16 = jnp.bfloat16

D_MODEL = 1024
GROUP = 256
N_HEADS = 4
HEAD_DIM = 64
DIFF_QK = 32
NORM_EPS = 1e-6
ALIBI_MAX_EXP = 8.0
SSM_STATE = 128
SSM_CONV = 4
SSM_CHUNK = 128
RWKV_CHUNK = 64
RWKV_PROJ = 896
RWKV_LN_EPS = 64e-5
D_FF = 2816
FFN_CONV = 3
FFN_CHUNK = 1408
N_MAIN = 3 * GROUP + 4 * GROUP + 3 * GROUP + RWKV_PROJ
N_SMALL = 128
ATT_BLOCK = 256
VMEM_LIMIT = 56 * 1024 * 1024


def _dot(a, b):
    return jnp.dot(a, b, preferred_element_type=F32)


def _dot_nt(a, b):
    return lax.dot_general(a, b, (((1,), (1,)), ((), ())), preferred_element_type=F32)


def _dot_tn(a, b):
    return lax.dot_general(a, b, (((0,), (0,)), ((), ())), preferred_element_type=F32)


def _split3(x):
    h1 = x.astype(BF16)
    r1 = x - h1.astype(F32)
    h2 = r1.astype(BF16)
    h3 = (r1 - h2.astype(F32)).astype(BF16)
    return h1, h2, h3


def _dot3_left(m_bf16, x):
    h1, h2, h3 = _split3(x)
    return _dot(m_bf16, h1) + _dot(m_bf16, h2) + _dot(m_bf16, h3)


def _dot3_right(x, m_bf16):
    h1, h2, h3 = _split3(x)
    return _dot(h1, m_bf16) + _dot(h2, m_bf16) + _dot(h3, m_bf16)


def _iota(shape, dim):
    return lax.broadcasted_iota(jnp.int32, shape, dim)


def _div(x, n):
    return lax.shift_right_logical(x, int(math.log2(n)))


def _mod(x, n):
    return lax.bitwise_and(x, n - 1)


def _softplus(x):
    return jnp.maximum(x, 0.0) + jnp.log(1.0 + jnp.exp(-jnp.abs(x)))


def _sigmoid(x):
    return 1.0 / (1.0 + jnp.exp(-x))


def _silu(x):
    return x * _sigmoid(x)


def _params(*sem):
    return pltpu.CompilerParams(dimension_semantics=sem, vmem_limit_bytes=VMEM_LIMIT)


def _in_proj_kernel(x_ref, g_ref, w_ref, ws_ref, sc_ref, od_ref, os_ref, of_ref, or_ref, osm_ref):
    x = x_ref[...]
    ms = jnp.mean(x * x, axis=-1, keepdims=True)
    h = (x * lax.rsqrt(ms + NORM_EPS) * g_ref[...]).astype(BF16)
    c0, c1, c2, c3 = 768, 1792, 2560, N_MAIN
    od_ref[...] = (_dot(h, w_ref[:, 0:c0]) * sc_ref[:, 0:c0]).astype(BF16)
    os_ref[...] = _dot(h, w_ref[:, c0:c1]).astype(BF16)
    of_ref[...] = (_dot(h, w_ref[:, c1:c2]) * sc_ref[:, c1:c2]).astype(BF16)
    or_ref[...] = _dot(h, w_ref[:, c2:c3]).astype(BF16)
    osm_ref[...] = _dot(h, ws_ref[...])


def _in_proj(x2, g, w_main, w_small, colscale, tm=512):
    t = x2.shape[0]
    full = lambda shape: pl.BlockSpec(shape, lambda i: (0, 0))
    rows = lambda n: pl.BlockSpec((tm, n), lambda i: (i, 0))
    return pl.pallas_call(
        _in_proj_kernel,
        grid=(t // tm,),
        in_specs=[rows(D_MODEL), full((1, D_MODEL)), full((D_MODEL, N_MAIN)), full((D_MODEL, N_SMALL)),
                  full((1, N_MAIN))],
        out_specs=[rows(768), rows(1024), rows(768), rows(RWKV_PROJ), rows(N_SMALL)],
        out_shape=[jax.ShapeDtypeStruct((t, 768), BF16), jax.ShapeDtypeStruct((t, 1024), BF16),
                   jax.ShapeDtypeStruct((t, 768), BF16), jax.ShapeDtypeStruct((t, RWKV_PROJ), BF16),
                   jax.ShapeDtypeStruct((t, N_SMALL), F32)],
        compiler_params=_params("parallel"),
        name="in_proj",
    )(x2, g, w_main, w_small, colscale)


def _fox_prep_kernel(ft_ref, b_ref, c_ref, *, blk):
    s = ft_ref.shape[2]
    tri = (_iota((blk, blk), 0) <= _iota((blk, blk), 1)).astype(BF16)
    carry = jnp.zeros((8, 1), F32)
    for j in range(s // blk):
        x = ft_ref[0, :, j * blk:(j + 1) * blk] + b_ref[...]
        ls = jnp.minimum(x, 0.0) - jnp.log(1.0 + jnp.exp(-jnp.abs(x)))
        cs = _dot3_right(ls, tri) + carry
        c_ref[0, :, j * blk:(j + 1) * blk] = cs
        carry = cs[:, blk - 1:blk]


def _fox_prep(ft, bias8):
    b, _, s = ft.shape
    blk = min(512, s)
    return pl.pallas_call(
        functools.partial(_fox_prep_kernel, blk=blk),
        grid=(b,),
        in_specs=[pl.BlockSpec((1, 8, s), lambda i: (i, 0, 0)), pl.BlockSpec((8, 1), lambda i: (0, 0))],
        out_specs=pl.BlockSpec((1, 8, s), lambda i: (i, 0, 0)),
        out_shape=jax.ShapeDtypeStruct((b, 8, s), F32),
        compiler_params=_params("parallel"),
        name="fox_prep",
    )(ft, bias8)


def _head_mean_matrix():
    same = _div(_iota((GROUP, GROUP), 0), HEAD_DIM) == _div(_iota((GROUP, GROUP), 1), HEAD_DIM)
    return jnp.where(same, 1.0 / HEAD_DIM, 0.0).astype(BF16)


def _head_mean(x):
    avg = _head_mean_matrix()
    h1 = x.astype(BF16)
    h2 = (x - h1.astype(F32)).astype(BF16)
    return _dot(h1, avg) + _dot(h2, avg)


def _attn_kernel(*refs, kind, tq):
    if kind == "diff":
        (q_ref, k_ref, v_ref, lq1_ref, lk1_ref, lq2_ref, lk2_ref, g_ref, o_ref,
         qs_ref, m_ref, l_ref, acc_ref) = refs
        n_maps, width = 2 * N_HEADS, DIFF_QK
    else:
        q_ref, k_ref, v_ref, c_ref, g_ref, o_ref, qs_ref, m_ref, l_ref, acc_ref = refs
        n_maps, width = N_HEADS, HEAD_DIM
    tk = tq
    rows = n_maps * tq
    i = pl.program_id(1)

    q = q_ref[0]
    lane = _div(_iota((1, GROUP), 1), width)
    for m in range(n_maps):
        qs_ref[m * tq:(m + 1) * tq, :] = jnp.where(lane == m, q, jnp.zeros_like(q))
    m_ref[...] = jnp.full(m_ref.shape, -jnp.inf, F32)
    l_ref[...] = jnp.zeros(l_ref.shape, F32)
    acc_ref[...] = jnp.zeros(acc_ref.shape, F32)

    r_loc = _mod(_iota((rows, tk), 0), tq)
    c_loc = _iota((rows, tk), 1)
    if kind == "diff":
        head = _div(_iota((rows, 1), 0), 2 * tq)
        slope = jnp.exp2(-(ALIBI_MAX_EXP / N_HEADS) * (head + 1).astype(F32))
        rel = (r_loc - c_loc).astype(F32)

    def step(j, masked):
        start = pl.multiple_of(j * tk, tk)
        k = k_ref[0, pl.ds(start, tk), :]
        v = v_ref[0, pl.ds(start, tk), :]
        s = _dot_nt(qs_ref[...], k)
        if kind == "diff":
            dist = rel + ((i - j) * tq).astype(F32)
            z = s - slope * dist
        else:
            crow = c_ref[0, j]
            bias = jnp.concatenate(
                [jnp.broadcast_to(crow[4 + h:5 + h, :], (tq, tk)) for h in range(N_HEADS)], axis=0)
            z = s - bias
        if masked:
            z = jnp.where(r_loc >= c_loc, z, -jnp.inf)
        m_prev = m_ref[...]
        m_new = jnp.maximum(m_prev, jnp.max(z, axis=1, keepdims=True))
        alpha = jnp.exp(m_prev - m_new)
        p = jnp.exp(z - m_new[:, 0:1])
        l_ref[...] = alpha * l_ref[...] + jnp.sum(p, axis=1, keepdims=True)
        acc_ref[...] = acc_ref[...] * alpha[:, 0:1] + _dot(p.astype(BF16), v)
        m_ref[...] = m_new

    def body(j, carry):
        step(j, False)
        return carry

    lax.fori_loop(0, i, body, 0)
    step(i, True)

    inv_l = 1.0 / l_ref[...][:, 0:1]
    hl = _div(_iota((1, GROUP), 1), HEAD_DIM)
    o = jnp.zeros((tq, GROUP), F32)
    if kind == "diff":
        lam_init = g_ref[1:2, 0:1]
        lam = (jnp.exp(jnp.sum(lq1_ref[...] * lk1_ref[...], axis=1, keepdims=True))
               - jnp.exp(jnp.sum(lq2_ref[...] * lk2_ref[...], axis=1, keepdims=True)) + lam_init)
        for h in range(N_HEADS):
            a0 = slice(2 * h * tq, (2 * h + 1) * tq)
            a1 = slice((2 * h + 1) * tq, (2 * h + 2) * tq)
            oh = acc_ref[a0, :] * inv_l[a0] - lam * (acc_ref[a1, :] * inv_l[a1])
            o = o + jnp.where(hl == h, oh, 0.0)
    else:
        for h in range(N_HEADS):
            a0 = slice(h * tq, (h + 1) * tq)
            o = o + jnp.where(hl == h, acc_ref[a0, :] * inv_l[a0], 0.0)
    ms = _head_mean(o * o)
    y = o * lax.rsqrt(ms + NORM_EPS) * g_ref[0:1, :]
    if kind == "diff":
        y = y * (1.0 - lam_init)
    o_ref[0] = y.astype(o_ref.dtype)


def _attention(kind, qkv, extra, gains, tq):
    b, s, _ = qkv.shape
    nq = s // tq
    n_maps = 2 * N_HEADS if kind == "diff" else N_HEADS
    rows = n_maps * tq
    qspec = pl.BlockSpec((1, tq, GROUP), lambda bi, i: (bi, i, 0))
    kspec = pl.BlockSpec((1, s, GROUP), lambda bi, i: (bi, 0, 1))
    vspec = pl.BlockSpec((1, s, GROUP), lambda bi, i: (bi, 0, 2))
    if kind == "diff":
        lspec = pl.BlockSpec((1, DIFF_QK), lambda bi, i: (0, 0))
        extra_specs = [lspec] * 4
    else:
        extra_specs = [pl.BlockSpec((1, nq, 8, tq), lambda bi, i: (bi, 0, 0, 0))]
    gspec = pl.BlockSpec((8, GROUP), lambda bi, i: (0, 0))
    return pl.pallas_call(
        functools.partial(_attn_kernel, kind=kind, tq=tq),
        grid=(b, nq),
        in_specs=[qspec, kspec, vspec] + extra_specs + [gspec],
        out_specs=pl.BlockSpec((1, tq, GROUP), lambda bi, i: (bi, i, 0)),
        out_shape=jax.ShapeDtypeStruct((b, s, GROUP), BF16),
        scratch_shapes=[pltpu.VMEM((rows, GROUP), BF16), pltpu.VMEM((rows, 128), F32),
                        pltpu.VMEM((rows, 128), F32), pltpu.VMEM((rows, GROUP), F32)],
        compiler_params=_params("parallel", "arbitrary"),
        name=kind + "_attention",
    )(qkv, qkv, qkv, *extra, gains)


def _ssd_kernel(u_ref, sm_ref, cw_ref, cb_ref, dtb_ref, a_ref, d_ref, g_ref, o_ref,
                prev_ref, state_ref):
    L = SSM_CHUNK
    c = pl.program_id(1)

    @pl.when(c == 0)
    def _():
        prev_ref[...] = jnp.zeros(prev_ref.shape, F32)
        state_ref[...] = jnp.zeros(state_ref.shape, F32)

    u = u_ref[0].astype(F32)
    z = u[:, 0:GROUP]
    raw = u[:, GROUP:]
    prev = prev_ref[...]
    row = _iota((L, 1), 0)
    xbc = raw * cw_ref[SSM_CONV - 1:SSM_CONV, :] + cb_ref[...]
    for d in range(1, SSM_CONV):
        shifted = jnp.where(row >= d, pltpu.roll(raw, d, 0), pltpu.roll(prev, d, 0))
        xbc = xbc + shifted * cw_ref[SSM_CONV - 1 - d:SSM_CONV - d, :]
    prev_ref[...] = raw
    xbc = _silu(xbc)
    xs = xbc[:, 0:GROUP]
    bm = xbc[:, GROUP:2 * GROUP]
    cm = xbc[:, 2 * GROUP:3 * GROUP]

    dt = _softplus(sm_ref[0] + dtb_ref[...])
    da = dt * a_ref[...]
    rep = (_iota((128, 4 * 128), 0) == _div(_iota((128, 4 * 128), 1), 128)).astype(BF16)
    exp_h = (_iota((128, GROUP), 0) == _div(_iota((128, GROUP), 1), HEAD_DIM)).astype(BF16)
    dt_e = _dot3_right(dt, exp_h)
    da_rep = _dot3_right(da, rep)
    tri = (_iota((L, L), 0) >= _iota((L, L), 1)).astype(BF16)
    acs = _dot3_left(tri, da_rep)
    last = acs[L - 1:L, :]
    causal = _iota((L, L), 0) >= _iota((L, L), 1)

    xdt = (xs * dt_e).astype(BF16)
    scores, bd, cd = [], [], []
    for h in range(N_HEADS):
        g = h // 2
        col = acs[:, h * 128:(h + 1) * 128]
        b_g = bm[:, g * 128:(g + 1) * 128]
        c_g = cm[:, g * 128:(g + 1) * 128]
        if h % 2 == 0:
            cb = _dot_nt(c_g.astype(BF16), b_g.astype(BF16))
        decay = jnp.exp(jnp.where(causal, col - col.T, -jnp.inf))
        scores.append((cb * decay).astype(BF16))
        bd.append((b_g * jnp.exp(last[:, h * 128:(h + 1) * 128] - col)).astype(BF16))
        cd.append((c_g * jnp.exp(col)).astype(BF16))
    y_st = _dot(jnp.concatenate(scores, axis=0), xdt)
    hl = _div(_iota((1, GROUP), 1), HEAD_DIM)
    y = jnp.zeros((L, GROUP), F32)
    for h in range(N_HEADS):
        y = y + jnp.where(hl == h, y_st[h * L:(h + 1) * L, :], 0.0)
    state = state_ref[...]
    y = y + _dot(jnp.concatenate(cd, axis=1), state.astype(BF16))
    st_new = _dot_tn(jnp.concatenate(bd, axis=1), xdt)
    blk = _div(_iota((4 * 128, GROUP), 0), 128) == _div(_iota((4 * 128, GROUP), 1), HEAD_DIM)
    chunk_decay = jnp.concatenate(
        [jnp.broadcast_to(jnp.exp(last[:, h * 128:h * 128 + 1]), (128, GROUP)) for h in range(N_HEADS)], axis=0)
    state_ref[...] = state * chunk_decay + jnp.where(blk, st_new, 0.0)

    y = y + xs * d_ref[...]
    yg = y * _silu(z)
    sq = yg * yg
    half = GROUP // 2
    ms0 = jnp.mean(sq[:, 0:half], axis=1, keepdims=True)
    ms1 = jnp.mean(sq[:, half:], axis=1, keepdims=True)
    ms = jnp.where(_iota((1, GROUP), 1) < half, ms0, ms1)
    o_ref[0] = (yg * lax.rsqrt(ms + NORM_EPS) * g_ref[...]).astype(o_ref.dtype)


def _ssd(u, small, conv_w, conv_b, dt_bias, a_neg, d_e, norm_g):
    b, s, _ = u.shape
    L = SSM_CHUNK
    cdim = 3 * GROUP
    full = lambda shape: pl.BlockSpec(shape, lambda bi, c: (0,) * len(shape))
    return pl.pallas_call(
        _ssd_kernel,
        grid=(b, s // L),
        in_specs=[pl.BlockSpec((1, L, 4 * GROUP), lambda bi, c: (bi, c, 0)),
                  pl.BlockSpec((1, L, N_SMALL), lambda bi, c: (bi, c, 0)),
                  full((SSM_CONV, cdim)), full((1, cdim)), full((1, N_SMALL)), full((1, N_SMALL)),
                  full((1, GROUP)), full((1, GROUP))],
        out_specs=pl.BlockSpec((1, L, GROUP), lambda bi, c: (bi, c, 0)),
        out_shape=jax.ShapeDtypeStruct((b, s, GROUP), BF16),
        scratch_shapes=[pltpu.VMEM((L, cdim), F32), pltpu.VMEM((4 * SSM_STATE, GROUP), F32)],
        compiler_params=_params("parallel", "arbitrary"),
        name="ssd_mixer",
    )(u, small, conv_w, conv_b, dt_bias, a_neg, d_e, norm_g)


def _rwkv_kernel(p_ref, mu_ref, w0_ref, w2_ref, a0_ref, a2_ref, g2_ref, kk_ref, ka_ref, rk_ref,
                 lnw_ref, lnb_ref, o_ref, last_ref, h_ref):
    C = RWKV_CHUNK
    R = N_HEADS * C
    c = pl.program_id(1)

    @pl.when(c == 0)
    def _():
        last_ref[...] = jnp.zeros(last_ref.shape, F32)
        h_ref[...] = jnp.zeros(h_ref.shape, F32)

    p = p_ref[0].astype(F32)
    row = _iota((C, 1), 0)
    p_prev = jnp.where(row >= 1, pltpu.roll(p, 1, 0), jnp.broadcast_to(last_ref[0:1, :], p.shape))
    last_ref[0:1, :] = p[C - 1:C, :]
    pm = p + (p_prev - p) * mu_ref[...]
    r = pm[:, 0:GROUP]
    k = pm[:, GROUP:2 * GROUP]
    v = pm[:, 2 * GROUP:3 * GROUP]
    xlow = pm[:, 3 * GROUP:]
    w_raw = -_softplus(-(w0_ref[...] + _dot(jnp.tanh(xlow).astype(BF16), w2_ref[...]))) - 0.5
    logw = -jnp.exp(w_raw)
    a = _sigmoid(a0_ref[...] + _dot(xlow.astype(BF16), a2_ref[...]))
    gate = _dot(_sigmoid(xlow).astype(BF16), g2_ref[...])
    kk = k * kk_ref[...]
    kk = kk * lax.rsqrt(_head_mean(kk * kk) * HEAD_DIM + 1e-12)
    k = k * (1.0 + (a - 1.0) * ka_ref[...])
    bonus = (_head_mean(r * k * rk_ref[...]) * HEAD_DIM) * v

    tri = (_iota((C, C), 0) >= _iota((C, C), 1)).astype(BF16)
    lc = _dot3_left(tri, logw)
    e_pos = jnp.exp(lc)
    e_neg = jnp.exp(-lc)
    pc = e_pos[C - 1:C, :]
    a_t = -kk * jnp.exp(lc - logw)
    b_t = (a * kk) * e_neg
    k_t = k * e_neg
    r_t = r * e_pos

    hl = _div(_iota((1, GROUP), 1), HEAD_DIM)

    def stack(x):
        return jnp.concatenate([jnp.where(hl == h, x, 0.0) for h in range(N_HEADS)], axis=0)

    a_s, b_s, k_s, r_s, v_s = stack(a_t), stack(b_t), stack(k_t), stack(r_t), stack(v)
    v_sb = v_s.astype(BF16)
    m_all = _dot_nt(jnp.concatenate([a_s, r_s], axis=0).astype(BF16),
                    jnp.concatenate([b_s, k_s], axis=0).astype(BF16))
    ri = _iota((R, R), 0)
    ci = _iota((R, R), 1)
    same = _div(ri, C) == _div(ci, C)
    strict = same & (_mod(ri, C) > _mod(ci, C))
    incl = same & (_mod(ri, C) >= _mod(ci, C))
    l_m = jnp.where(strict, m_all[0:R, 0:R], 0.0)
    m_ak = jnp.where(strict, m_all[0:R, R:], 0.0)
    m_rb = jnp.where(incl, m_all[R:, 0:R], 0.0)
    m_rk = jnp.where(incl, m_all[R:, R:], 0.0)

    t_m = jnp.where(ri == ci, 1.0, 0.0) + l_m
    l_p = l_m
    for _ in range(int(math.log2(C)) - 1):
        l_pb = l_p.astype(BF16)
        l_p = _dot(l_pb, l_pb)
        t_m = t_m + _dot(t_m.astype(BF16), l_p.astype(BF16))
    t_b = t_m.astype(BF16)
    w_s = _dot(t_b, a_s.astype(BF16))
    u0 = _dot(t_b, _dot(m_ak.astype(BF16), v_sb).astype(BF16))
    w_sb = w_s.astype(BF16)
    u0_b = u0.astype(BF16)
    bh = (b_s * pc).astype(BF16)
    kh = (k_s * pc).astype(BF16)
    g_m = jnp.where(ri == ci, jnp.broadcast_to(pc, (R, R)), 0.0) + _dot_tn(bh, w_sb)
    d_m = _dot_tn(bh, u0_b) + _dot_tn(kh, v_sb)
    q_m = r_s + _dot(m_rb.astype(BF16), w_sb)
    y0 = _dot(m_rb.astype(BF16), u0_b) + _dot(m_rk.astype(BF16), v_sb)
    h_b = h_ref[...].astype(BF16)
    y_s = _dot(q_m.astype(BF16), h_b) + y0
    h_ref[...] = _dot(g_m.astype(BF16), h_b) + d_m
    y = y_s[0:C] + y_s[C:2 * C] + y_s[2 * C:3 * C] + y_s[3 * C:4 * C]

    mean = _head_mean(y)
    yc = y - mean
    var = _head_mean(yc * yc)
    o = yc * lax.rsqrt(var + RWKV_LN_EPS) * lnw_ref[...] + lnb_ref[...] + bonus
    o_ref[0] = (o * gate).astype(o_ref.dtype)


def _rwkv(p, mu, w0, w2p, a0, a2p, g2p, k_k, k_a, r_k, ln_w, ln_b):
    b, s, _ = p.shape
    C = RWKV_CHUNK
    full = lambda shape: pl.BlockSpec(shape, lambda bi, c: (0,) * len(shape))
    vec = full((1, GROUP))
    mat = full((128, GROUP))
    return pl.pallas_call(
        _rwkv_kernel,
        grid=(b, s // C),
        in_specs=[pl.BlockSpec((1, C, RWKV_PROJ), lambda bi, c: (bi, c, 0)), full((1, RWKV_PROJ)),
                  vec, mat, vec, mat, mat, vec, vec, vec, vec, vec],
        out_specs=pl.BlockSpec((1, C, GROUP), lambda bi, c: (bi, c, 0)),
        out_shape=jax.ShapeDtypeStruct((b, s, GROUP), BF16),
        scratch_shapes=[pltpu.VMEM((8, RWKV_PROJ), F32), pltpu.VMEM((GROUP, GROUP), F32)],
        compiler_params=_params("parallel", "arbitrary"),
        name="rwkv7_mix",
    )(p, mu, w0, w2p, a0, a2p, g2p, k_k, k_a, r_k, ln_w, ln_b)


def _out_proj_kernel(x_ref, y0_ref, y1_ref, y2_ref, y3_ref, w_ref, g_ref, o_ref):
    acc = _dot(y0_ref[...], w_ref[0:GROUP, :])
    acc = acc + _dot(y1_ref[...], w_ref[GROUP:2 * GROUP, :])
    acc = acc + _dot(y2_ref[...], w_ref[2 * GROUP:3 * GROUP, :])
    acc = acc + _dot(y3_ref[...], w_ref[3 * GROUP:4 * GROUP, :])
    ms = jnp.mean(acc * acc, axis=-1, keepdims=True)
    o_ref[...] = x_ref[...] + acc * lax.rsqrt(ms + NORM_EPS) * g_ref[...]


def _out_proj(x2, ys, w, g, tm=512):
    t = x2.shape[0]
    rows = lambda n: pl.BlockSpec((tm, n), lambda i: (i, 0))
    full = lambda shape: pl.BlockSpec(shape, lambda i: (0, 0))
    return pl.pallas_call(
        _out_proj_kernel,
        grid=(t // tm,),
        in_specs=[rows(D_MODEL)] + [rows(GROUP)] * 4 + [full((D_MODEL, D_MODEL)), full((1, D_MODEL))],
        out_specs=rows(D_MODEL),
        out_shape=jax.ShapeDtypeStruct((t, D_MODEL), F32),
        compiler_params=_params("parallel"),
        name="out_proj",
    )(x2, *ys, w, g)


def _ffn_kernel(x_ref, g1_ref, wg_ref, wu_ref, cw_ref, cb_ref, wd_ref, g2_ref, o_ref, tail_ref):
    tm = x_ref.shape[1]
    j = pl.program_id(1)

    @pl.when(j == 0)
    def _():
        tail_ref[...] = jnp.zeros(tail_ref.shape, F32)

    x = x_ref[0]
    ms = jnp.mean(x * x, axis=-1, keepdims=True)
    h = (x * lax.rsqrt(ms + NORM_EPS) * g1_ref[...]).astype(BF16)
    row8 = _iota((8, 1), 0)
    acc = jnp.zeros((tm, D_MODEL), F32)
    for c in range(D_FF // FFN_CHUNK):
        cs = slice(c * FFN_CHUNK, (c + 1) * FFN_CHUNK)
        gate = _dot(h, wg_ref[:, cs])
        up = _dot(h, wu_ref[:, cs])
        tail = tail_ref[:, cs]
        conv = gate * cw_ref[FFN_CONV - 1:FFN_CONV, cs] + cb_ref[:, cs]
        for d in range(1, FFN_CONV):
            rolled = pltpu.roll(gate, d, 0)
            head = jnp.where(row8 < d, pltpu.roll(tail, d, 0), rolled[0:8])
            shifted = jnp.concatenate([head, rolled[8:]], axis=0)
            conv = conv + shifted * cw_ref[FFN_CONV - 1 - d:FFN_CONV - d, cs]
        tail_ref[:, cs] = gate[tm - 8:tm, :]
        inner = 0.7978845608028654 * (conv + 0.044715 * conv * conv * conv)
        act = 0.5 * conv * (1.0 + jnp.tanh(inner))
        acc = acc + _dot((act * up).astype(BF16), wd_ref[cs, :])
    ms2 = jnp.mean(acc * acc, axis=-1, keepdims=True)
    o_ref[0] = x + acc * lax.rsqrt(ms2 + NORM_EPS) * g2_ref[...]


def _ffn(x, g1, wg, wu, cw, cb, wd, g2, tm=512):
    b, s, _ = x.shape
    tm = min(tm, s)
    full = lambda shape: pl.BlockSpec(shape, lambda bi, j: (0, 0), pipeline_mode=pl.Buffered(1))
    blk = pl.BlockSpec((1, tm, D_MODEL), lambda bi, j: (bi, j, 0))
    return pl.pallas_call(
        _ffn_kernel,
        grid=(b, s // tm),
        in_specs=[blk, full((1, D_MODEL)), full((D_MODEL, D_FF)), full((D_MODEL, D_FF)),
                  full((FFN_CONV, D_FF)), full((1, D_FF)), full((D_FF, D_MODEL)), full((1, D_MODEL))],
        out_specs=blk,
        out_shape=jax.ShapeDtypeStruct((b, s, D_MODEL), F32),
        scratch_shapes=[pltpu.VMEM((8, D_FF), F32)],
        compiler_params=_params("parallel", "arbitrary"),
        name="conv_glu_ffn",
    )(x, g1, wg, wu, cw, cb, wd, g2)


def _row(v):
    return v.reshape(1, -1).astype(F32)


def _pad_rows(m, start, total=128):
    return jnp.zeros((total, m.shape[1]), m.dtype).at[start:start + m.shape[0]].set(m)


def kernel(x, norm_mix_pre, norm_mix_post, norm_ffn_pre, norm_ffn_post, w_in, w_out, diff_lambda_q1, diff_lambda_k1, diff_lambda_q2, diff_lambda_k2, diff_subln, ssm_conv_w, ssm_conv_b, ssm_dt_bias, ssm_a_log, ssm_d, ssm_norm, fox_f_bias, fox_norm, rwkv_mu, rwkv_w0, rwkv_w2, rwkv_a0, rwkv_a2, rwkv_g2, rwkv_k_k, rwkv_k_a, rwkv_r_k, rwkv_ln_w, rwkv_ln_b, ffn_w_gate, ffn_w_up, ffn_conv_w, ffn_conv_b, ffn_w_down):
    b, s, d = x.shape
    t = b * s
    depth = w_in.shape[0]
    tq = min(ATT_BLOCK, s)
    o_dt = 768 + 256 + 768
    o_fq = o_dt + 4
    o_ff = o_fq + 768
    o_rp = o_ff + 4
    colscale = jnp.ones((1, N_MAIN), F32)
    colscale = colscale.at[:, 0:256].set(DIFF_QK ** -0.5).at[:, 1792:2048].set(HEAD_DIM ** -0.5)
    for l in range(depth):
        wl = w_in[l]
        w_main = jnp.concatenate([wl[:, 0:o_dt], wl[:, o_fq:o_ff], wl[:, o_rp:]], axis=1).astype(BF16)
        w_small = jnp.concatenate([wl[:, o_dt:o_fq], wl[:, o_ff:o_rp],
                                   jnp.zeros((d, N_SMALL - 8), F32)], axis=1).astype(BF16)
        u_diff, u_ssm, u_fox, u_rwkv, u_small = _in_proj(
            x.reshape(t, d), _row(norm_mix_pre[l]), w_main, w_small, colscale)
        u_diff = u_diff.reshape(b, s, -1)
        u_ssm = u_ssm.reshape(b, s, -1)
        u_fox = u_fox.reshape(b, s, -1)
        u_rwkv = u_rwkv.reshape(b, s, -1)
        u_small = u_small.reshape(b, s, -1)

        lam_init = 0.8 - 0.6 * math.exp(-0.3 * l)
        g_diff = jnp.zeros((8, GROUP), F32).at[0].set(jnp.tile(diff_subln[l], N_HEADS)).at[1].set(lam_init)
        y_diff = _attention("diff", u_diff,
                            [_row(diff_lambda_q1[l]), _row(diff_lambda_k1[l]),
                             _row(diff_lambda_q2[l]), _row(diff_lambda_k2[l])], g_diff, tq)

        small_t = jnp.swapaxes(u_small[:, :, 0:8], 1, 2)
        bias8 = jnp.concatenate([jnp.zeros((4,), F32), fox_f_bias[l]]).reshape(8, 1)
        c_t = _fox_prep(small_t, bias8)
        c_blk = jnp.swapaxes(c_t.reshape(b, 8, s // tq, tq), 1, 2)
        g_fox = jnp.zeros((8, GROUP), F32).at[0].set(jnp.tile(fox_norm[l], N_HEADS))
        y_fox = _attention("fox", u_fox, [c_blk], g_fox, tq)

        pad4 = lambda v: jnp.zeros((1, N_SMALL), F32).at[0, 0:4].set(v)
        y_ssm = _ssd(u_ssm, u_small, ssm_conv_w[l], _row(ssm_conv_b[l]), pad4(ssm_dt_bias[l]),
                     pad4(-jnp.exp(ssm_a_log[l])), _row(jnp.repeat(ssm_d[l], HEAD_DIM)), _row(ssm_norm[l]))

        y_rwkv = _rwkv(u_rwkv, _row(rwkv_mu[l]), _row(rwkv_w0[l]),
                       _pad_rows(rwkv_w2[l], 0).astype(BF16), _row(rwkv_a0[l]),
                       _pad_rows(rwkv_a2[l], 32).astype(BF16), _pad_rows(rwkv_g2[l], 64).astype(BF16),
                       _row(rwkv_k_k[l]), _row(rwkv_k_a[l]), _row(rwkv_r_k[l]),
                       _row(rwkv_ln_w[l]), _row(rwkv_ln_b[l]))

        flat = lambda y: y.reshape(t, GROUP)
        x = _out_proj(x.reshape(t, d), [flat(y_diff), flat(y_ssm), flat(y_fox), flat(y_rwkv)],
                      w_out[l].astype(BF16), _row(norm_mix_post[l])).reshape(b, s, d)

        x = _ffn(x, _row(norm_ffn_pre[l]), ffn_w_gate[l].astype(BF16), ffn_w_up[l].astype(BF16),
                 ffn_conv_w[l], _row(ffn_conv_b[l]), ffn_w_down[l].astype(BF16), _row(norm_ffn_post[l]))
    return x
```

```python
import functools
import math

import jax
import jax.numpy as jnp
from jax import lax
from jax.experimental import pallas as pl
from jax.experimental.pallas import tpu as pltpu

F32 = jnp.float32
BF16 = jnp.bfloat16

D_MODEL = 1024
GROUP = 256
N_HEADS = 4
HEAD_DIM = 64
DIFF_QK = 32
NORM_EPS = 1e-6
ALIBI_MAX_EXP = 8.0
SSM_STATE = 128
SSM_CONV = 4
SSM_CHUNK = 128
RWKV_CHUNK = 64
RWKV_PROJ = 896
RWKV_LN_EPS = 64e-5
D_FF = 2816
FFN_CONV = 3
FFN_CHUNK = 1408
AUG = 2 * GROUP
AUG_LANE = HEAD_DIM
COLS = (0, AUG, 2 * AUG, 2 * AUG + GROUP, 2 * AUG + 5 * GROUP, 3 * AUG + 5 * GROUP, 4 * AUG + 5 * GROUP,
        4 * AUG + 6 * GROUP, 4 * AUG + 6 * GROUP + RWKV_PROJ)
N_MAIN = COLS[-1]
N_SMALL = 128
ATT_BLOCK = 512
VMEM_LIMIT = 56 * 1024 * 1024


def _dot(a, b):
    return jnp.dot(a, b, preferred_element_type=F32)


def _dot_nt(a, b):
    return lax.dot_general(a, b, (((1,), (1,)), ((), ())), preferred_element_type=F32)


def _dot_tn(a, b):
    return lax.dot_general(a, b, (((0,), (0,)), ((), ())), preferred_element_type=F32)


def _split3(x):
    h1 = x.astype(BF16)
    r1 = x - h1.astype(F32)
    h2 = r1.astype(BF16)
    h3 = (r1 - h2.astype(F32)).astype(BF16)
    return h1, h2, h3


def _dot3_left(m_bf16, x):
    h1, h2, h3 = _split3(x)
    return _dot(m_bf16, h1) + _dot(m_bf16, h2) + _dot(m_bf16, h3)


def _dot3_right(x, m_bf16):
    h1, h2, h3 = _split3(x)
    return _dot(h1, m_bf16) + _dot(h2, m_bf16) + _dot(h3, m_bf16)


def _iota(shape, dim):
    return lax.broadcasted_iota(jnp.int32, shape, dim)


def _div(x, n):
    return lax.shift_right_logical(x, int(math.log2(n)))


def _mod(x, n):
    return lax.bitwise_and(x, n - 1)


def _softplus(x):
    return jnp.maximum(x, 0.0) + jnp.log(1.0 + jnp.exp(-jnp.abs(x)))


def _sigmoid(x):
    return 1.0 / (1.0 + jnp.exp(-x))


def _silu(x):
    return x * _sigmoid(x)


def _params(*sem):
    return pltpu.CompilerParams(dimension_semantics=sem, vmem_limit_bytes=VMEM_LIMIT)


def _in_proj_kernel(x_ref, g_ref, w_ref, ws_ref, aug_ref, fb_ref, place_ref,
                    qd_ref, kd_ref, vd_ref, os_ref, qf_ref, kf_ref, vf_ref, or_ref, osm_ref,
                    carry_ref, *, tm, tq, blocks_per_seq):
    i = pl.program_id(0)

    @pl.when(i % blocks_per_seq == 0)
    def _():
        carry_ref[...] = jnp.zeros(carry_ref.shape, F32)

    x = x_ref[...]
    ms = jnp.mean(x * x, axis=-1, keepdims=True)
    h = (x * lax.rsqrt(ms + NORM_EPS) * g_ref[...]).astype(BF16)
    proj = lambda n: _dot(h, w_ref[:, COLS[n]:COLS[n + 1]])
    pos = _mod(_iota((tm, 1), 0), tq)
    lo = _mod(pos, 256).astype(F32)
    hi = pos.astype(F32) - lo
    qd_ref[...] = (proj(0) * aug_ref[0:1, :] + lo * aug_ref[1:2, :] + hi * aug_ref[2:3, :]
                   + aug_ref[3:4, :]).astype(BF16)
    kd_ref[...] = (proj(1) + lo * aug_ref[4:5, :] + hi * aug_ref[5:6, :] + aug_ref[6:7, :]).astype(BF16)
    vd_ref[...] = proj(2).astype(BF16)
    os_ref[...] = proj(3).astype(BF16)
    small = _dot(h, ws_ref[...])
    osm_ref[...] = small
    f = small + fb_ref[...]
    ls = jnp.minimum(f, 0.0) - jnp.log(1.0 + jnp.exp(-jnp.abs(f)))
    tri = (_iota((tm, tm), 0) >= _iota((tm, tm), 1)).astype(BF16)
    cum = _dot3_left(tri, ls) + carry_ref[0:1, :]
    carry_ref[0:1, :] = cum[tm - 1:tm, :]
    c1, c2, c3 = _split3(cum)
    kbias = _dot(c1, place_ref[0]) + _dot(c2, place_ref[1]) + _dot(c3, place_ref[2])
    qf_ref[...] = (proj(4) * aug_ref[7:8, :] + aug_ref[8:9, :]).astype(BF16)
    kf_ref[...] = (proj(5) + kbias).astype(BF16)
    vf_ref[...] = proj(6).astype(BF16)
    or_ref[...] = proj(7).astype(BF16)


def _in_proj(x2, g, w_main, w_small, aug_rows, fbias, place, seq, tq, tm=512):
    t = x2.shape[0]
    tm = min(tm, seq)
    full = lambda shape: pl.BlockSpec(shape, lambda i: (0,) * len(shape))
    rows = lambda n: pl.BlockSpec((tm, n), lambda i: (i, 0))
    widths = (AUG, AUG, GROUP, 4 * GROUP, AUG, AUG, GROUP, RWKV_PROJ)
    return pl.pallas_call(
        functools.partial(_in_proj_kernel, tm=tm, tq=tq, blocks_per_seq=seq // tm),
        grid=(t // tm,),
        in_specs=[rows(D_MODEL), full((1, D_MODEL)), full((D_MODEL, N_MAIN)), full((D_MODEL, N_SMALL)),
                  full((16, AUG)), full((1, N_SMALL)), full((3, N_SMALL, AUG))],
        out_specs=[rows(n) for n in widths] + [rows(N_SMALL)],
        out_shape=[jax.ShapeDtypeStruct((t, n), BF16) for n in widths]
        + [jax.ShapeDtypeStruct((t, N_SMALL), F32)],
        scratch_shapes=[pltpu.VMEM((8, N_SMALL), F32)],
        compiler_params=_params("arbitrary"),
        name="in_proj",
    )(x2, g, w_main, w_small, aug_rows, fbias, place)


def _head_mean_matrix():
    same = _div(_iota((GROUP, GROUP), 0), HEAD_DIM) == _div(_iota((GROUP, GROUP), 1), HEAD_DIM)
    return jnp.where(same, 1.0 / HEAD_DIM, 0.0).astype(BF16)


def _head_mean(x):
    avg = _head_mean_matrix()
    h1 = x.astype(BF16)
    h2 = (x - h1.astype(F32)).astype(BF16)
    return _dot(h1, avg) + _dot(h2, avg)


def _attn_kernel(*refs, kind, tq):
    if kind == "diff":
        (qt_ref, k_ref, vt_ref, lq1_ref, lk1_ref, lq2_ref, lk2_ref, g_ref, o_ref,
         qs_ref, m_ref, l_ref, acc_ref) = refs
        n_c = 2
    else:
        qt_ref, k_ref, vt_ref, g_ref, o_ref, qs_ref, m_ref, l_ref, acc_ref = refs
        n_c = 1
    tk = tq
    i = pl.program_id(1)

    sub = _iota((128, 1), 0)
    for h in range(N_HEADS):
        qh = qt_ref[0, h * 128:(h + 1) * 128, :]
        if kind == "diff":
            zero = jnp.zeros_like(qh)
            qs_ref[h, :, 0:tq] = jnp.where((sub >= DIFF_QK) & (sub < 2 * DIFF_QK), zero, qh)
            qs_ref[h, :, tq:2 * tq] = jnp.where(sub < DIFF_QK, zero, qh)
        else:
            qs_ref[h] = qh
    m_ref[...] = jnp.full(m_ref.shape, -jnp.inf, F32)
    l_ref[...] = jnp.zeros(l_ref.shape, F32)
    acc_ref[...] = jnp.zeros(acc_ref.shape, F32)
    kpos = _iota((tk, n_c * tq), 0)
    qpos = _mod(_iota((tk, n_c * tq), 1), tq)
    causal = kpos <= qpos

    def step(j, masked):
        start = pl.multiple_of(j * tk, tk)
        block_dist = ((i - j) * tq).astype(F32)
        for h in range(N_HEADS):
            k_h = k_ref[0, pl.ds(start, tk), h * 128:(h + 1) * 128]
            v_h = vt_ref[0, j, h * HEAD_DIM:(h + 1) * HEAD_DIM, :]
            off = -(2.0 ** (-(ALIBI_MAX_EXP / N_HEADS) * (h + 1))) * block_dist if kind == "diff" else 0.0
            s = _dot(k_h, qs_ref[h])
            if masked:
                s = jnp.where(causal, s, -jnp.inf)
            m_old = m_ref[h]
            m_new = jnp.maximum(m_old, jnp.max(s, axis=0, keepdims=True) + off)
            alpha = jnp.exp(m_old - m_new)
            p = jnp.exp(s - (m_new - off))
            l_ref[h] = alpha * l_ref[h] + jnp.sum(p, axis=0, keepdims=True)
            acc_ref[h] = acc_ref[h] * alpha + _dot(v_h, p.astype(BF16))
            m_ref[h] = m_new

    def body(j, carry):
        step(j, False)
        return carry

    lax.fori_loop(0, i, body, 0)
    step(i, True)

    parts = []
    if kind == "diff":
        lam_init = g_ref[1:2, 0:1]
        lam = (jnp.exp(jnp.sum(lq1_ref[...] * lk1_ref[...], axis=1, keepdims=True))
               - jnp.exp(jnp.sum(lq2_ref[...] * lk2_ref[...], axis=1, keepdims=True)) + lam_init)
        for h in range(N_HEADS):
            on = acc_ref[h] * (1.0 / l_ref[h])
            parts.append(on[:, 0:tq] - lam * on[:, tq:2 * tq])
    else:
        for h in range(N_HEADS):
            parts.append(acc_ref[h] * (1.0 / l_ref[h]))
    o = jnp.concatenate(parts, axis=0).T
    ms = _head_mean(o * o)
    y = o * lax.rsqrt(ms + NORM_EPS) * g_ref[0:1, :]
    if kind == "diff":
        y = y * (1.0 - lam_init)
    o_ref[0] = y.astype(o_ref.dtype)


def _attention(kind, q_t, k_aug, v_t, extra, gains, tq):
    b, s, _ = k_aug.shape
    nq = s // tq
    w = (2 if kind == "diff" else 1) * tq
    extra_specs = [pl.BlockSpec((1, DIFF_QK), lambda bi, i: (0, 0))] * len(extra)
    once = pl.Buffered(1)
    return pl.pallas_call(
        functools.partial(_attn_kernel, kind=kind, tq=tq),
        grid=(b, nq),
        in_specs=[pl.BlockSpec((1, AUG, tq), lambda bi, i: (bi, 0, i)),
                  pl.BlockSpec((1, s, AUG), lambda bi, i: (bi, 0, 0), pipeline_mode=once),
                  pl.BlockSpec((1, nq, GROUP, tq), lambda bi, i: (bi, 0, 0, 0), pipeline_mode=once)]
        + extra_specs + [pl.BlockSpec((8, GROUP), lambda bi, i: (0, 0))],
        out_specs=pl.BlockSpec((1, tq, GROUP), lambda bi, i: (bi, i, 0)),
        out_shape=jax.ShapeDtypeStruct((b, s, GROUP), BF16),
        scratch_shapes=[pltpu.VMEM((N_HEADS, 128, w), BF16), pltpu.VMEM((N_HEADS, 1, w), F32),
                        pltpu.VMEM((N_HEADS, 1, w), F32), pltpu.VMEM((N_HEADS, HEAD_DIM, w), F32)],
        compiler_params=_params("parallel", "arbitrary"),
        name=kind + "_attention",
    )(q_t, k_aug, v_t, *extra, gains)


def _ssd_kernel(u_ref, sm_ref, cw_ref, cb_ref, dtb_ref, a_ref, d_ref, g_ref, o_ref,
                prev_ref, state_ref):
    L = SSM_CHUNK
    c = pl.program_id(1)

    @pl.when(c == 0)
    def _():
        prev_ref[...] = jnp.zeros(prev_ref.shape, F32)
        state_ref[...] = jnp.zeros(state_ref.shape, F32)

    u = u_ref[0].astype(F32)
    z = u[:, 0:GROUP]
    raw = u[:, GROUP:]
    prev = prev_ref[...]
    row = _iota((L, 1), 0)
    xbc = raw * cw_ref[SSM_CONV - 1:SSM_CONV, :] + cb_ref[...]
    for d in range(1, SSM_CONV):
        shifted = jnp.where(row >= d, pltpu.roll(raw, d, 0), pltpu.roll(prev, d, 0))
        xbc = xbc + shifted * cw_ref[SSM_CONV - 1 - d:SSM_CONV - d, :]
    prev_ref[...] = raw
    xbc = _silu(xbc)
    xs = xbc[:, 0:GROUP]
    bm = xbc[:, GROUP:2 * GROUP]
    cm = xbc[:, 2 * GROUP:3 * GROUP]

    dt = _softplus(sm_ref[0] + dtb_ref[...])
    da = dt * a_ref[...]
    rep = (_iota((128, 4 * 128), 0) == _div(_iota((128, 4 * 128), 1), 128)).astype(BF16)
    exp_h = (_iota((128, GROUP), 0) == _div(_iota((128, GROUP), 1), HEAD_DIM)).astype(BF16)
    dt_e = _dot3_right(dt, exp_h)
    da_rep = _dot3_right(da, rep)
    tri = (_iota((L, L), 0) >= _iota((L, L), 1)).astype(BF16)
    acs = _dot3_left(tri, da_rep)
    last = acs[L - 1:L, :]
    causal = _iota((L, L), 0) >= _iota((L, L), 1)

    xdt = (xs * dt_e).astype(BF16)
    scores, bd, cd = [], [], []
    for h in range(N_HEADS):
        g = h // 2
        col = acs[:, h * 128:(h + 1) * 128]
        b_g = bm[:, g * 128:(g + 1) * 128]
        c_g = cm[:, g * 128:(g + 1) * 128]
        if h % 2 == 0:
            cb = _dot_nt(c_g.astype(BF16), b_g.astype(BF16))
        decay = jnp.exp(jnp.where(causal, col - col.T, -jnp.inf))
        scores.append((cb * decay).astype(BF16))
        bd.append((b_g * jnp.exp(last[:, h * 128:(h + 1) * 128] - col)).astype(BF16))
        cd.append((c_g * jnp.exp(col)).astype(BF16))
    y_st = _dot(jnp.concatenate(scores, axis=0), xdt)
    hl = _div(_iota((1, GROUP), 1), HEAD_DIM)
    y = jnp.zeros((L, GROUP), F32)
    for h in range(N_HEADS):
        y = y + jnp.where(hl == h, y_st[h * L:(h + 1) * L, :], 0.0)
    state = state_ref[...]
    y = y + _dot(jnp.concatenate(cd, axis=1), state.astype(BF16))
    st_new = _dot_tn(jnp.concatenate(bd, axis=1), xdt)
    blk = _div(_iota((4 * 128, GROUP), 0), 128) == _div(_iota((4 * 128, GROUP), 1), HEAD_DIM)
    chunk_decay = jnp.concatenate(
        [jnp.broadcast_to(jnp.exp(last[:, h * 128:h * 128 + 1]), (128, GROUP)) for h in range(N_HEADS)], axis=0)
    state_ref[...] = state * chunk_decay + jnp.where(blk, st_new, 0.0)

    y = y + xs * d_ref[...]
    yg = y * _silu(z)
    sq = yg * yg
    half = GROUP // 2
    ms0 = jnp.mean(sq[:, 0:half], axis=1, keepdims=True)
    ms1 = jnp.mean(sq[:, half:], axis=1, keepdims=True)
    ms = jnp.where(_iota((1, GROUP), 1) < half, ms0, ms1)
    o_ref[0] = (yg * lax.rsqrt(ms + NORM_EPS) * g_ref[...]).astype(o_ref.dtype)


def _ssd(u, small, conv_w, conv_b, dt_bias, a_neg, d_e, norm_g):
    b, s, _ = u.shape
    L = SSM_CHUNK
    cdim = 3 * GROUP
    full = lambda shape: pl.BlockSpec(shape, lambda bi, c: (0,) * len(shape))
    return pl.pallas_call(
        _ssd_kernel,
        grid=(b, s // L),
        in_specs=[pl.BlockSpec((1, L, 4 * GROUP), lambda bi, c: (bi, c, 0)),
                  pl.BlockSpec((1, L, N_SMALL), lambda bi, c: (bi, c, 0)),
                  full((SSM_CONV, cdim)), full((1, cdim)), full((1, N_SMALL)), full((1, N_SMALL)),
                  full((1, GROUP)), full((1, GROUP))],
        out_specs=pl.BlockSpec((1, L, GROUP), lambda bi, c: (bi, c, 0)),
        out_shape=jax.ShapeDtypeStruct((b, s, GROUP), BF16),
        scratch_shapes=[pltpu.VMEM((L, cdim), F32), pltpu.VMEM((4 * SSM_STATE, GROUP), F32)],
        compiler_params=_params("parallel", "arbitrary"),
        name="ssd_mixer",
    )(u, small, conv_w, conv_b, dt_bias, a_neg, d_e, norm_g)


def _rwkv_kernel(p_ref, mu_ref, w0_ref, w2_ref, a0_ref, a2_ref, g2_ref, kk_ref, ka_ref, rk_ref,
                 lnw_ref, lnb_ref, o_ref, last_ref, h_ref):
    C = RWKV_CHUNK
    R = N_HEADS * C
    c = pl.program_id(1)

    @pl.when(c == 0)
    def _():
        last_ref[...] = jnp.zeros(last_ref.shape, F32)
        h_ref[...] = jnp.zeros(h_ref.shape, F32)

    p = p_ref[0].astype(F32)
    row = _iota((C, 1), 0)
    p_prev = jnp.where(row >= 1, pltpu.roll(p, 1, 0), jnp.broadcast_to(last_ref[0:1, :], p.shape))
    last_ref[0:1, :] = p[C - 1:C, :]
    pm = p + (p_prev - p) * mu_ref[...]
    r = pm[:, 0:GROUP]
    k = pm[:, GROUP:2 * GROUP]
    v = pm[:, 2 * GROUP:3 * GROUP]
    xlow = pm[:, 3 * GROUP:]
    w_raw = -_softplus(-(w0_ref[...] + _dot(jnp.tanh(xlow).astype(BF16), w2_ref[...]))) - 0.5
    logw = -jnp.exp(w_raw)
    a = _sigmoid(a0_ref[...] + _dot(xlow.astype(BF16), a2_ref[...]))
    gate = _dot(_sigmoid(xlow).astype(BF16), g2_ref[...])
    kk = k * kk_ref[...]
    kk = kk * lax.rsqrt(_head_mean(kk * kk) * HEAD_DIM + 1e-12)
    k = k * (1.0 + (a - 1.0) * ka_ref[...])
    bonus = (_head_mean(r * k * rk_ref[...]) * HEAD_DIM) * v

    tri = (_iota((C, C), 0) >= _iota((C, C), 1)).astype(BF16)
    lc = _dot3_left(tri, logw)
    e_pos = jnp.exp(lc)
    e_neg = jnp.exp(-lc)
    pc = e_pos[C - 1:C, :]
    a_t = -kk * jnp.exp(lc - logw)
    b_t = (a * kk) * e_neg
    k_t = k * e_neg
    r_t = r * e_pos

    hl = _div(_iota((1, GROUP), 1), HEAD_DIM)

    def stack(x):
        return jnp.concatenate([jnp.where(hl == h, x, 0.0) for h in range(N_HEADS)], axis=0)

    a_s, b_s, k_s, r_s, v_s = stack(a_t), stack(b_t), stack(k_t), stack(r_t), stack(v)
    v_sb = v_s.astype(BF16)
    m_all = _dot_nt(jnp.concatenate([a_s, r_s], axis=0).astype(BF16),
                    jnp.concatenate([b_s, k_s], axis=0).astype(BF16))
    ri = _iota((R, R), 0)
    ci = _iota((R, R), 1)
    same = _div(ri, C) == _div(ci, C)
    strict = same & (_mod(ri, C) > _mod(ci, C))
    incl = same & (_mod(ri, C) >= _mod(ci, C))
    l_m = jnp.where(strict, m_all[0:R, 0:R], 0.0)
    m_ak = jnp.where(strict, m_all[0:R, R:], 0.0)
    m_rb = jnp.where(incl, m_all[R:, 0:R], 0.0)
    m_rk = jnp.where(incl, m_all[R:, R:], 0.0)

    t_m = jnp.where(ri == ci, 1.0, 0.0) + l_m
    l_p = l_m
    for _ in range(int(math.log2(C)) - 1):
        l_pb = l_p.astype(BF16)
        l_p = _dot(l_pb, l_pb)
        t_m = t_m + _dot(t_m.astype(BF16), l_p.astype(BF16))
    t_b = t_m.astype(BF16)
    w_s = _dot(t_b, a_s.astype(BF16))
    u0 = _dot(t_b, _dot(m_ak.astype(BF16), v_sb).astype(BF16))
    w_sb = w_s.astype(BF16)
    u0_b = u0.astype(BF16)
    bh = (b_s * pc).astype(BF16)
    kh = (k_s * pc).astype(BF16)
    g_m = jnp.where(ri == ci, jnp.broadcast_to(pc, (R, R)), 0.0) + _dot_tn(bh, w_sb)
    d_m = _dot_tn(bh, u0_b) + _dot_tn(kh, v_sb)
    q_m = r_s + _dot(m_rb.astype(BF16), w_sb)
    y0 = _dot(m_rb.astype(BF16), u0_b) + _dot(m_rk.astype(BF16), v_sb)
    h_b = h_ref[...].astype(BF16)
    y_s = _dot(q_m.astype(BF16), h_b) + y0
    h_ref[...] = _dot(g_m.astype(BF16), h_b) + d_m
    y = y_s[0:C] + y_s[C:2 * C] + y_s[2 * C:3 * C] + y_s[3 * C:4 * C]

    mean = _head_mean(y)
    yc = y - mean
    var = _head_mean(yc * yc)
    o = yc * lax.rsqrt(var + RWKV_LN_EPS) * lnw_ref[...] + lnb_ref[...] + bonus
    o_ref[0] = (o * gate).astype(o_ref.dtype)


def _rwkv(p, mu, w0, w2p, a0, a2p, g2p, k_k, k_a, r_k, ln_w, ln_b):
    b, s, _ = p.shape
    C = RWKV_CHUNK
    full = lambda shape: pl.BlockSpec(shape, lambda bi, c: (0,) * len(shape))
    vec = full((1, GROUP))
    mat = full((128, GROUP))
    return pl.pallas_call(
        _rwkv_kernel,
        grid=(b, s // C),
        in_specs=[pl.BlockSpec((1, C, RWKV_PROJ), lambda bi, c: (bi, c, 0)), full((1, RWKV_PROJ)),
                  vec, mat, vec, mat, mat, vec, vec, vec, vec, vec],
        out_specs=pl.BlockSpec((1, C, GROUP), lambda bi, c: (bi, c, 0)),
        out_shape=jax.ShapeDtypeStruct((b, s, GROUP), BF16),
        scratch_shapes=[pltpu.VMEM((8, RWKV_PROJ), F32), pltpu.VMEM((GROUP, GROUP), F32)],
        compiler_params=_params("parallel", "arbitrary"),
        name="rwkv7_mix",
    )(p, mu, w0, w2p, a0, a2p, g2p, k_k, k_a, r_k, ln_w, ln_b)


def _out_proj_kernel(x_ref, y0_ref, y1_ref, y2_ref, y3_ref, w_ref, g_ref, o_ref):
    acc = _dot(y0_ref[...], w_ref[0:GROUP, :])
    acc = acc + _dot(y1_ref[...], w_ref[GROUP:2 * GROUP, :])
    acc = acc + _dot(y2_ref[...], w_ref[2 * GROUP:3 * GROUP, :])
    acc = acc + _dot(y3_ref[...], w_ref[3 * GROUP:4 * GROUP, :])
    ms = jnp.mean(acc * acc, axis=-1, keepdims=True)
    o_ref[...] = x_ref[...] + acc * lax.rsqrt(ms + NORM_EPS) * g_ref[...]


def _out_proj(x2, ys, w, g, tm=512):
    t = x2.shape[0]
    rows = lambda n: pl.BlockSpec((tm, n), lambda i: (i, 0))
    full = lambda shape: pl.BlockSpec(shape, lambda i: (0, 0))
    return pl.pallas_call(
        _out_proj_kernel,
        grid=(t // tm,),
        in_specs=[rows(D_MODEL)] + [rows(GROUP)] * 4 + [full((D_MODEL, D_MODEL)), full((1, D_MODEL))],
        out_specs=rows(D_MODEL),
        out_shape=jax.ShapeDtypeStruct((t, D_MODEL), F32),
        compiler_params=_params("parallel"),
        name="out_proj",
    )(x2, *ys, w, g)


def _ffn_kernel(x_ref, g1_ref, wg_ref, wu_ref, cw_ref, cb_ref, wd_ref, g2_ref, o_ref, tail_ref):
    tm = x_ref.shape[1]
    j = pl.program_id(1)

    @pl.when(j == 0)
    def _():
        tail_ref[...] = jnp.zeros(tail_ref.shape, F32)

    x = x_ref[0]
    ms = jnp.mean(x * x, axis=-1, keepdims=True)
    h = (x * lax.rsqrt(ms + NORM_EPS) * g1_ref[...]).astype(BF16)
    row8 = _iota((8, 1), 0)
    acc = jnp.zeros((tm, D_MODEL), F32)
    for c in range(D_FF // FFN_CHUNK):
        cs = slice(c * FFN_CHUNK, (c + 1) * FFN_CHUNK)
        gate = _dot(h, wg_ref[:, cs])
        up = _dot(h, wu_ref[:, cs])
        tail = tail_ref[:, cs]
        conv = gate * cw_ref[FFN_CONV - 1:FFN_CONV, cs] + cb_ref[:, cs]
        for d in range(1, FFN_CONV):
            rolled = pltpu.roll(gate, d, 0)
            head = jnp.where(row8 < d, pltpu.roll(tail, d, 0), rolled[0:8])
            shifted = jnp.concatenate([head, rolled[8:]], axis=0)
            conv = conv + shifted * cw_ref[FFN_CONV - 1 - d:FFN_CONV - d, cs]
        tail_ref[:, cs] = gate[tm - 8:tm, :]
        inner = 0.7978845608028654 * (conv + 0.044715 * conv * conv * conv)
        act = 0.5 * conv * (1.0 + jnp.tanh(inner))
        acc = acc + _dot((act * up).astype(BF16), wd_ref[cs, :])
    ms2 = jnp.mean(acc * acc, axis=-1, keepdims=True)
    o_ref[0] = x + acc * lax.rsqrt(ms2 + NORM_EPS) * g2_ref[...]


def _ffn(x, g1, wg, wu, cw, cb, wd, g2, tm=512):
    b, s, _ = x.shape
    tm = min(tm, s)
    full = lambda shape: pl.BlockSpec(shape, lambda bi, j: (0, 0), pipeline_mode=pl.Buffered(1))
    blk = pl.BlockSpec((1, tm, D_MODEL), lambda bi, j: (bi, j, 0))
    return pl.pallas_call(
        _ffn_kernel,
        grid=(b, s // tm),
        in_specs=[blk, full((1, D_MODEL)), full((D_MODEL, D_FF)), full((D_MODEL, D_FF)),
                  full((FFN_CONV, D_FF)), full((1, D_FF)), full((D_FF, D_MODEL)), full((1, D_MODEL))],
        out_specs=blk,
        out_shape=jax.ShapeDtypeStruct((b, s, D_MODEL), F32),
        scratch_shapes=[pltpu.VMEM((8, D_FF), F32)],
        compiler_params=_params("parallel", "arbitrary"),
        name="conv_glu_ffn",
    )(x, g1, wg, wu, cw, cb, wd, g2)


def _row(v):
    return v.reshape(1, -1).astype(F32)


def _pad_rows(m, start, total=128):
    return jnp.zeros((total, m.shape[1]), m.dtype).at[start:start + m.shape[0]].set(m)


def _head_groups(w):
    d = w.shape[0]
    w4 = w.reshape(d, N_HEADS, HEAD_DIM)
    return jnp.concatenate([w4, jnp.zeros_like(w4)], axis=2).reshape(d, AUG)


def _bias_lane_constants():
    lanes = jnp.arange(AUG)
    head, local = lanes // 128, lanes % 128
    slope = jnp.exp2(-(ALIBI_MAX_EXP / N_HEADS) * (head + 1).astype(F32))
    at = lambda k: (local == AUG_LANE + k).astype(F32)
    zero = jnp.zeros((AUG,), F32)
    fox_q = -(at(0) + at(1) + at(2))
    rows = jnp.stack([jnp.full((AUG,), DIFF_QK ** -0.5, F32), -slope * at(0), -slope * at(1),
                      slope * (at(2) + at(3)), at(2), at(3), at(0) + at(1),
                      jnp.full((AUG,), HEAD_DIM ** -0.5, F32), fox_q] + [zero] * 7)
    src = jnp.arange(N_SMALL)
    place = jnp.stack([((src[:, None] == 4 + head[None, :]) & (local[None, :] == AUG_LANE + k)).astype(BF16)
                       for k in range(3)])
    return rows, place


def kernel(x, norm_mix_pre, norm_mix_post, norm_ffn_pre, norm_ffn_post, w_in, w_out, diff_lambda_q1, diff_lambda_k1, diff_lambda_q2, diff_lambda_k2, diff_subln, ssm_conv_w, ssm_conv_b, ssm_dt_bias, ssm_a_log, ssm_d, ssm_norm, fox_f_bias, fox_norm, rwkv_mu, rwkv_w0, rwkv_w2, rwkv_a0, rwkv_a2, rwkv_g2, rwkv_k_k, rwkv_k_a, rwkv_r_k, rwkv_ln_w, rwkv_ln_b, ffn_w_gate, ffn_w_up, ffn_conv_w, ffn_conv_b, ffn_w_down):
    b, s, d = x.shape
    t = b * s
    depth = w_in.shape[0]
    tq = min(ATT_BLOCK, s)
    nq = s // tq
    o_dt = 768 + 256 + 768
    o_fq = o_dt + 4
    o_ff = o_fq + 768
    o_rp = o_ff + 4
    aug_rows, place = _bias_lane_constants()
    to_qt = lambda u: jnp.swapaxes(u.reshape(b, s, AUG), 1, 2)
    to_vt = lambda u: jnp.swapaxes(u.reshape(b, nq, tq, GROUP), 2, 3)
    for l in range(depth):
        wl = w_in[l]
        w_main = jnp.concatenate(
            [_head_groups(wl[:, 0:256]), _head_groups(wl[:, 256:512]), wl[:, 512:o_dt],
             _head_groups(wl[:, o_fq:o_fq + 256]), _head_groups(wl[:, o_fq + 256:o_fq + 512]),
             wl[:, o_fq + 512:o_ff], wl[:, o_rp:]], axis=1).astype(BF16)
        w_small = jnp.concatenate([wl[:, o_dt:o_fq], wl[:, o_ff:o_rp],
                                   jnp.zeros((d, N_SMALL - 8), F32)], axis=1).astype(BF16)
        fbias = jnp.zeros((1, N_SMALL), F32).at[0, 4:8].set(fox_f_bias[l])
        (qd, kd, vd, u_ssm, qf, kf, vf, u_rwkv, u_small) = _in_proj(
            x.reshape(t, d), _row(norm_mix_pre[l]), w_main, w_small, aug_rows, fbias, place, s, tq)
        u_ssm = u_ssm.reshape(b, s, -1)
        u_rwkv = u_rwkv.reshape(b, s, -1)
        u_small = u_small.reshape(b, s, -1)

        lam_init = 0.8 - 0.6 * math.exp(-0.3 * l)
        g_diff = jnp.zeros((8, GROUP), F32).at[0].set(jnp.tile(diff_subln[l], N_HEADS)).at[1].set(lam_init)
        y_diff = _attention("diff", to_qt(qd), kd.reshape(b, s, AUG), to_vt(vd),
                            [_row(diff_lambda_q1[l]), _row(diff_lambda_k1[l]),
                             _row(diff_lambda_q2[l]), _row(diff_lambda_k2[l])], g_diff, tq)
        g_fox = jnp.zeros((8, GROUP), F32).at[0].set(jnp.tile(fox_norm[l], N_HEADS))
        y_fox = _attention("fox", to_qt(qf), kf.reshape(b, s, AUG), to_vt(vf), [], g_fox, tq)

        pad4 = lambda v: jnp.zeros((1, N_SMALL), F32).at[0, 0:4].set(v)
        y_ssm = _ssd(u_ssm, u_small, ssm_conv_w[l], _row(ssm_conv_b[l]), pad4(ssm_dt_bias[l]),
                     pad4(-jnp.exp(ssm_a_log[l])), _row(jnp.repeat(ssm_d[l], HEAD_DIM)), _row(ssm_norm[l]))

        y_rwkv = _rwkv(u_rwkv, _row(rwkv_mu[l]), _row(rwkv_w0[l]),
                       _pad_rows(rwkv_w2[l], 0).astype(BF16), _row(rwkv_a0[l]),
                       _pad_rows(rwkv_a2[l], 32).astype(BF16), _pad_rows(rwkv_g2[l], 64).astype(BF16),
                       _row(rwkv_k_k[l]), _row(rwkv_k_a[l]), _row(rwkv_r_k[l]),
                       _row(rwkv_ln_w[l]), _row(rwkv_ln_b[l]))

        flat = lambda y: y.reshape(t, GROUP)
        x = _out_proj(x.reshape(t, d), [flat(y_diff), flat(y_ssm), flat(y_fox), flat(y_rwkv)],
                      w_out[l].astype(BF16), _row(norm_mix_post[l])).reshape(b, s, d)

        x = _ffn(x, _row(norm_ffn_pre[l]), ffn_w_gate[l].astype(BF16), ffn_w_up[l].astype(BF16),
                 ffn_conv_w[l], _row(ffn_conv_b[l]), ffn_w_down[l].astype(BF16), _row(norm_ffn_post[l]))
    return x
```

```python
import functools
import math

import jax
import jax.numpy as jnp
from jax import lax
from jax.experimental import pallas as pl
from jax.experimental.pallas import tpu as pltpu

F32 = jnp.float32
BF16 = jnp.bfloat16

D_MODEL = 1024
GROUP = 256
N_HEADS = 4
HEAD_DIM = 64
DIFF_QK = 32
NORM_EPS = 1e-6
ALIBI_MAX_EXP = 8.0
SSM_STATE = 128
SSM_CONV = 4
SSM_CHUNK = 128
RWKV_CHUNK = 64
RWKV_CHUNKS_PER_STEP = 4
RWKV_PROJ = 896
RWKV_LN_EPS = 64e-5
D_FF = 2816
FFN_CONV = 3
FFN_CHUNK = 1408
AUG = 2 * GROUP
AUG_LANE = HEAD_DIM
V_ROWS = HEAD_DIM + 16
LOG2E = 1.4426950408889634
COLS = (0, AUG, 2 * AUG, 2 * AUG + GROUP, 2 * AUG + 5 * GROUP, 3 * AUG + 5 * GROUP, 4 * AUG + 5 * GROUP,
        4 * AUG + 6 * GROUP, 4 * AUG + 6 * GROUP + RWKV_PROJ)
N_MAIN = COLS[-1]
N_SMALL = 128
ATT_BLOCK = 512
ATT_UNIT_WIDTH = 512
VMEM_LIMIT = 56 * 1024 * 1024


def _dot(a, b):
    return jnp.dot(a, b, preferred_element_type=F32)


def _dot_nt(a, b):
    return lax.dot_general(a, b, (((1,), (1,)), ((), ())), preferred_element_type=F32)


def _dot_tn(a, b):
    return lax.dot_general(a, b, (((0,), (0,)), ((), ())), preferred_element_type=F32)


def _split3(x):
    h1 = x.astype(BF16)
    r1 = x - h1.astype(F32)
    h2 = r1.astype(BF16)
    h3 = (r1 - h2.astype(F32)).astype(BF16)
    return h1, h2, h3


def _dot3_left(m_bf16, x):
    h1, h2, h3 = _split3(x)
    return _dot(m_bf16, h1) + _dot(m_bf16, h2) + _dot(m_bf16, h3)


def _dot3_right(x, m_bf16):
    h1, h2, h3 = _split3(x)
    return _dot(h1, m_bf16) + _dot(h2, m_bf16) + _dot(h3, m_bf16)


def _iota(shape, dim):
    return lax.broadcasted_iota(jnp.int32, shape, dim)


def _div(x, n):
    return lax.shift_right_logical(x, int(math.log2(n)))


def _mod(x, n):
    return lax.bitwise_and(x, n - 1)


def _softplus(x):
    return jnp.maximum(x, 0.0) + jnp.log(1.0 + jnp.exp(-jnp.abs(x)))


def _sigmoid(x):
    return 1.0 / (1.0 + jnp.exp(-x))


def _silu(x):
    return x * _sigmoid(x)


def _params(*sem):
    return pltpu.CompilerParams(dimension_semantics=sem, vmem_limit_bytes=VMEM_LIMIT)


def _in_proj_kernel(x_ref, g_ref, w_ref, ws_ref, aug_ref, fb_ref, place_ref,
                    qd_ref, kd_ref, vd_ref, os_ref, qf_ref, kf_ref, vf_ref, or_ref, osm_ref,
                    carry_ref, *, tm, tq, blocks_per_seq):
    i = pl.program_id(0)

    @pl.when(i % blocks_per_seq == 0)
    def _():
        carry_ref[...] = jnp.zeros(carry_ref.shape, F32)

    x = x_ref[...]
    ms = jnp.mean(x * x, axis=-1, keepdims=True)
    h = (x * lax.rsqrt(ms + NORM_EPS) * g_ref[...]).astype(BF16)
    proj = lambda n: _dot(h, w_ref[:, COLS[n]:COLS[n + 1]])
    pos = _mod(_iota((tm, 1), 0), tq)
    lo = _mod(pos, 256).astype(F32)
    hi = pos.astype(F32) - lo
    qd_ref[...] = (proj(0) * aug_ref[0:1, :] + lo * aug_ref[1:2, :] + hi * aug_ref[2:3, :]
                   + aug_ref[3:4, :]).astype(BF16)
    kd_ref[...] = (proj(1) + lo * aug_ref[4:5, :] + hi * aug_ref[5:6, :] + aug_ref[6:7, :]).astype(BF16)
    vd_ref[...] = proj(2).astype(BF16)
    os_ref[...] = proj(3).astype(BF16)
    small = _dot(h, ws_ref[...])
    osm_ref[...] = small
    f = small + fb_ref[...]
    ls = jnp.minimum(f, 0.0) - jnp.log(1.0 + jnp.exp(-jnp.abs(f)))
    tri = (_iota((tm, tm), 0) >= _iota((tm, tm), 1)).astype(BF16)
    cum = _dot3_left(tri, ls) + carry_ref[0:1, :]
    carry_ref[0:1, :] = cum[tm - 1:tm, :]
    c1, c2, c3 = _split3(cum * LOG2E)
    kbias = _dot(c1, place_ref[0]) + _dot(c2, place_ref[1]) + _dot(c3, place_ref[2])
    qf_ref[...] = (proj(4) * aug_ref[7:8, :] + aug_ref[8:9, :]).astype(BF16)
    kf_ref[...] = (proj(5) + kbias).astype(BF16)
    vf_ref[...] = proj(6).astype(BF16)
    or_ref[...] = proj(7).astype(BF16)


def _in_proj(x2, g, w_main, w_small, aug_rows, fbias, place, seq, tq, tm=512):
    t = x2.shape[0]
    tm = min(tm, seq)
    full = lambda shape: pl.BlockSpec(shape, lambda i: (0,) * len(shape))
    rows = lambda n: pl.BlockSpec((tm, n), lambda i: (i, 0))
    widths = (AUG, AUG, GROUP, 4 * GROUP, AUG, AUG, GROUP, RWKV_PROJ)
    return pl.pallas_call(
        functools.partial(_in_proj_kernel, tm=tm, tq=tq, blocks_per_seq=seq // tm),
        grid=(t // tm,),
        in_specs=[rows(D_MODEL), full((1, D_MODEL)), full((D_MODEL, N_MAIN)), full((D_MODEL, N_SMALL)),
                  full((16, AUG)), full((1, N_SMALL)), full((3, N_SMALL, AUG))],
        out_specs=[rows(n) for n in widths] + [rows(N_SMALL)],
        out_shape=[jax.ShapeDtypeStruct((t, n), BF16) for n in widths]
        + [jax.ShapeDtypeStruct((t, N_SMALL), F32)],
        scratch_shapes=[pltpu.VMEM((8, N_SMALL), F32)],
        compiler_params=_params("arbitrary"),
        name="in_proj",
    )(x2, g, w_main, w_small, aug_rows, fbias, place)


def _head_mean_matrix():
    same = _div(_iota((GROUP, GROUP), 0), HEAD_DIM) == _div(_iota((GROUP, GROUP), 1), HEAD_DIM)
    return jnp.where(same, 1.0 / HEAD_DIM, 0.0).astype(BF16)


def _head_mean(x):
    avg = _head_mean_matrix()
    h1 = x.astype(BF16)
    h2 = (x - h1.astype(F32)).astype(BF16)
    return _dot(h1, avg) + _dot(h2, avg)


def _attn_kernel(*refs, kind, tq):
    if kind == "diff":
        (qt_ref, k_ref, vt_ref, lq1_ref, lk1_ref, lq2_ref, lk2_ref, g_ref, o_ref,
         qs_ref, m_ref, acc_ref) = refs
        n_c = 2
    else:
        qt_ref, k_ref, vt_ref, g_ref, o_ref, qs_ref, m_ref, acc_ref = refs
        n_c = 1
    tk = tq
    i = pl.program_id(1)

    sub = _iota((128, 1), 0)
    for h in range(N_HEADS):
        qh = qt_ref[0, h * 128:(h + 1) * 128, :]
        if kind == "diff":
            zero = jnp.zeros_like(qh)
            qs_ref[h, :, 0:tq] = jnp.where((sub >= DIFF_QK) & (sub < 2 * DIFF_QK), zero, qh)
            qs_ref[h, :, tq:2 * tq] = jnp.where(sub < DIFF_QK, zero, qh)
        else:
            qs_ref[h] = qh
    m_ref[...] = jnp.full(m_ref.shape, -jnp.inf, F32)
    acc_ref[...] = jnp.zeros(acc_ref.shape, F32)
    wu = min(ATT_UNIT_WIDTH, tq)
    units = [(h, l0) for h in range(N_HEADS) for l0 in range(0, n_c * tq, wu)]
    kpos = _iota((tk, wu), 0)
    qcol = _iota((tk, wu), 1)

    def step(j, masked):
        start = pl.multiple_of(j * tk, tk)
        block_dist = ((i - j) * tq).astype(F32)

        def logits(unit):
            h, l0 = unit
            return _dot(k_ref[0, pl.ds(start, tk), h * 128:(h + 1) * 128], qs_ref[h, :, l0:l0 + wu])

        s_next = logits(units[0])
        for n, (h, l0) in enumerate(units):
            s = s_next
            if n + 1 < len(units):
                s_next = logits(units[n + 1])
            off = (-LOG2E * 2.0 ** (-(ALIBI_MAX_EXP / N_HEADS) * (h + 1))) * block_dist if kind == "diff" else 0.0
            if masked:
                s = jnp.where(kpos <= qcol + (l0 % tq), s, -jnp.inf)
            m_old = m_ref[h, :, l0:l0 + wu]
            m_new = jnp.maximum(m_old, jnp.max(s, axis=0, keepdims=True) + off)
            alpha = jnp.exp2(m_old - m_new)
            p = jnp.exp2(s - (m_new - off)).astype(BF16)
            v_h = vt_ref[0, j, h * V_ROWS:(h + 1) * V_ROWS, :]
            acc_ref[h, :, l0:l0 + wu] = acc_ref[h, :, l0:l0 + wu] * alpha + _dot(v_h, p)
            m_ref[h, :, l0:l0 + wu] = m_new

    def body(j, carry):
        step(j, False)
        return carry

    lax.fori_loop(0, i, body, 0)
    step(i, True)

    parts = []
    if kind == "diff":
        lam_init = g_ref[1:2, 0:1]
        lam = (jnp.exp(jnp.sum(lq1_ref[...] * lk1_ref[...], axis=1, keepdims=True))
               - jnp.exp(jnp.sum(lq2_ref[...] * lk2_ref[...], axis=1, keepdims=True)) + lam_init)
        for h in range(N_HEADS):
            on = acc_ref[h, 0:HEAD_DIM, :] * (1.0 / acc_ref[h, HEAD_DIM:HEAD_DIM + 1, :])
            parts.append(on[:, 0:tq] - lam * on[:, tq:2 * tq])
    else:
        for h in range(N_HEADS):
            parts.append(acc_ref[h, 0:HEAD_DIM, :] * (1.0 / acc_ref[h, HEAD_DIM:HEAD_DIM + 1, :]))
    o = jnp.concatenate(parts, axis=0).T
    ms = _head_mean(o * o)
    y = o * lax.rsqrt(ms + NORM_EPS) * g_ref[0:1, :]
    if kind == "diff":
        y = y * (1.0 - lam_init)
    o_ref[0] = y.astype(o_ref.dtype)


def _attention(kind, q_t, k_aug, v_t, extra, gains, tq):
    b, s, _ = k_aug.shape
    nq = s // tq
    w = (2 if kind == "diff" else 1) * tq
    extra_specs = [pl.BlockSpec((1, DIFF_QK), lambda bi, i: (0, 0))] * len(extra)
    once = pl.Buffered(1)
    return pl.pallas_call(
        functools.partial(_attn_kernel, kind=kind, tq=tq),
        grid=(b, nq),
        in_specs=[pl.BlockSpec((1, AUG, tq), lambda bi, i: (bi, 0, i)),
                  pl.BlockSpec((1, s, AUG), lambda bi, i: (bi, 0, 0), pipeline_mode=once),
                  pl.BlockSpec((1, nq, N_HEADS * V_ROWS, tq), lambda bi, i: (bi, 0, 0, 0), pipeline_mode=once)]
        + extra_specs + [pl.BlockSpec((8, GROUP), lambda bi, i: (0, 0))],
        out_specs=pl.BlockSpec((1, tq, GROUP), lambda bi, i: (bi, i, 0)),
        out_shape=jax.ShapeDtypeStruct((b, s, GROUP), BF16),
        scratch_shapes=[pltpu.VMEM((N_HEADS, 128, w), BF16), pltpu.VMEM((N_HEADS, 1, w), F32),
                        pltpu.VMEM((N_HEADS, V_ROWS, w), F32)],
        compiler_params=_params("arbitrary", "arbitrary"),
        name=kind + "_attention",
    )(q_t, k_aug, v_t, *extra, gains)


def _ssd_kernel(u_ref, sm_ref, cw_ref, cb_ref, dtb_ref, a_ref, d_ref, g_ref, o_ref,
                prev_ref, state_ref):
    L = SSM_CHUNK
    c = pl.program_id(1)

    @pl.when(c == 0)
    def _():
        prev_ref[...] = jnp.zeros(prev_ref.shape, F32)
        state_ref[...] = jnp.zeros(state_ref.shape, F32)

    u = u_ref[0].astype(F32)
    z = u[:, 0:GROUP]
    raw = u[:, GROUP:]
    prev = prev_ref[...]
    row = _iota((L, 1), 0)
    xbc = raw * cw_ref[SSM_CONV - 1:SSM_CONV, :] + cb_ref[...]
    for d in range(1, SSM_CONV):
        shifted = jnp.where(row >= d, pltpu.roll(raw, d, 0), pltpu.roll(prev, d, 0))
        xbc = xbc + shifted * cw_ref[SSM_CONV - 1 - d:SSM_CONV - d, :]
    prev_ref[...] = raw
    xbc = _silu(xbc)
    xs = xbc[:, 0:GROUP]
    bm = xbc[:, GROUP:2 * GROUP]
    cm = xbc[:, 2 * GROUP:3 * GROUP]

    dt = _softplus(sm_ref[0] + dtb_ref[...])
    da = dt * a_ref[...]
    rep = (_iota((128, 4 * 128), 0) == _div(_iota((128, 4 * 128), 1), 128)).astype(BF16)
    exp_h = (_iota((128, GROUP), 0) == _div(_iota((128, GROUP), 1), HEAD_DIM)).astype(BF16)
    dt_e = _dot3_right(dt, exp_h)
    da_rep = _dot3_right(da, rep)
    tri = (_iota((L, L), 0) >= _iota((L, L), 1)).astype(BF16)
    acs = _dot3_left(tri, da_rep)
    last = acs[L - 1:L, :]
    causal = _iota((L, L), 0) >= _iota((L, L), 1)

    xdt = (xs * dt_e).astype(BF16)
    scores, bd, cd = [], [], []
    for h in range(N_HEADS):
        g = h // 2
        col = acs[:, h * 128:(h + 1) * 128]
        b_g = bm[:, g * 128:(g + 1) * 128]
        c_g = cm[:, g * 128:(g + 1) * 128]
        if h % 2 == 0:
            cb = _dot_nt(c_g.astype(BF16), b_g.astype(BF16))
        decay = jnp.exp(jnp.where(causal, col - col.T, -jnp.inf))
        scores.append((cb * decay).astype(BF16))
        bd.append((b_g * jnp.exp(last[:, h * 128:(h + 1) * 128] - col)).astype(BF16))
        cd.append((c_g * jnp.exp(col)).astype(BF16))
    y_st = _dot(jnp.concatenate(scores, axis=0), xdt)
    hl = _div(_iota((1, GROUP), 1), HEAD_DIM)
    y = jnp.zeros((L, GROUP), F32)
    for h in range(N_HEADS):
        y = y + jnp.where(hl == h, y_st[h * L:(h + 1) * L, :], 0.0)
    state = state_ref[...]
    y = y + _dot(jnp.concatenate(cd, axis=1), state.astype(BF16))
    st_new = _dot_tn(jnp.concatenate(bd, axis=1), xdt)
    blk = _div(_iota((4 * 128, GROUP), 0), 128) == _div(_iota((4 * 128, GROUP), 1), HEAD_DIM)
    chunk_decay = jnp.concatenate(
        [jnp.broadcast_to(jnp.exp(last[:, h * 128:h * 128 + 1]), (128, GROUP)) for h in range(N_HEADS)], axis=0)
    state_ref[...] = state * chunk_decay + jnp.where(blk, st_new, 0.0)

    y = y + xs * d_ref[...]
    yg = y * _silu(z)
    sq = yg * yg
    half = GROUP // 2
    ms0 = jnp.mean(sq[:, 0:half], axis=1, keepdims=True)
    ms1 = jnp.mean(sq[:, half:], axis=1, keepdims=True)
    ms = jnp.where(_iota((1, GROUP), 1) < half, ms0, ms1)
    o_ref[0] = (yg * lax.rsqrt(ms + NORM_EPS) * g_ref[...]).astype(o_ref.dtype)


def _ssd(u, small, conv_w, conv_b, dt_bias, a_neg, d_e, norm_g):
    b, s, _ = u.shape
    L = SSM_CHUNK
    cdim = 3 * GROUP
    full = lambda shape: pl.BlockSpec(shape, lambda bi, c: (0,) * len(shape))
    return pl.pallas_call(
        _ssd_kernel,
        grid=(b, s // L),
        in_specs=[pl.BlockSpec((1, L, 4 * GROUP), lambda bi, c: (bi, c, 0)),
                  pl.BlockSpec((1, L, N_SMALL), lambda bi, c: (bi, c, 0)),
                  full((SSM_CONV, cdim)), full((1, cdim)), full((1, N_SMALL)), full((1, N_SMALL)),
                  full((1, GROUP)), full((1, GROUP))],
        out_specs=pl.BlockSpec((1, L, GROUP), lambda bi, c: (bi, c, 0)),
        out_shape=jax.ShapeDtypeStruct((b, s, GROUP), BF16),
        scratch_shapes=[pltpu.VMEM((L, cdim), F32), pltpu.VMEM((4 * SSM_STATE, GROUP), F32)],
        compiler_params=_params("arbitrary", "arbitrary"),
        name="ssd_mixer",
    )(u, small, conv_w, conv_b, dt_bias, a_neg, d_e, norm_g)


def _rwkv_kernel(p_ref, mu_ref, w0_ref, w2_ref, a0_ref, a2_ref, g2_ref, kk_ref, ka_ref, rk_ref,
                 lnw_ref, lnb_ref, o_ref, last_ref, h_ref):
    C = RWKV_CHUNK
    R = N_HEADS * C
    T = p_ref.shape[1]
    n_chunks = T // C
    c = pl.program_id(1)

    @pl.when(c == 0)
    def _():
        last_ref[...] = jnp.zeros(last_ref.shape, F32)
        h_ref[...] = jnp.zeros(h_ref.shape, F32)

    p = p_ref[0].astype(F32)
    row = _iota((T, 1), 0)
    p_prev = jnp.where(row >= 1, pltpu.roll(p, 1, 0), jnp.broadcast_to(last_ref[0:1, :], p.shape))
    last_ref[0:1, :] = p[T - 1:T, :]
    pm = p + (p_prev - p) * mu_ref[...]
    r = pm[:, 0:GROUP]
    k = pm[:, GROUP:2 * GROUP]
    v = pm[:, 2 * GROUP:3 * GROUP]
    xlow = pm[:, 3 * GROUP:]
    w_raw = -_softplus(-(w0_ref[...] + _dot(jnp.tanh(xlow).astype(BF16), w2_ref[...]))) - 0.5
    logw = -jnp.exp(w_raw)
    a = _sigmoid(a0_ref[...] + _dot(xlow.astype(BF16), a2_ref[...]))
    gate = _dot(_sigmoid(xlow).astype(BF16), g2_ref[...])
    kk = k * kk_ref[...]
    kk = kk * lax.rsqrt(_head_mean(kk * kk) * HEAD_DIM + 1e-12)
    k = k * (1.0 + (a - 1.0) * ka_ref[...])
    bonus = (_head_mean(r * k * rk_ref[...]) * HEAD_DIM) * v

    ti = _iota((T, T), 0)
    tj = _iota((T, T), 1)
    tri = ((ti >= tj) & (_div(ti, C) == _div(tj, C))).astype(BF16)
    lc = _dot3_left(tri, logw)
    e_pos = jnp.exp(lc)
    e_neg = jnp.exp(-lc)
    a_t = -kk * jnp.exp(lc - logw)
    b_t = (a * kk) * e_neg
    k_t = k * e_neg
    r_t = r * e_pos

    hl = _div(_iota((1, GROUP), 1), HEAD_DIM)
    ri = _iota((R, R), 0)
    ci = _iota((R, R), 1)
    same = _div(ri, C) == _div(ci, C)
    strict = same & (_mod(ri, C) > _mod(ci, C))
    incl = same & (_mod(ri, C) >= _mod(ci, C))
    eye = ri == ci
    chunks = range(n_chunks)

    def stack(x, n):
        xc = x[n * C:(n + 1) * C]
        return jnp.concatenate([jnp.where(hl == h, xc, 0.0) for h in range(N_HEADS)], axis=0)

    a_s = [stack(a_t, n) for n in chunks]
    b_s = [stack(b_t, n) for n in chunks]
    k_s = [stack(k_t, n) for n in chunks]
    r_s = [stack(r_t, n) for n in chunks]
    v_sb = [stack(v, n).astype(BF16) for n in chunks]
    pc = [e_pos[n * C + C - 1:n * C + C, :] for n in chunks]
    m_all = [_dot_nt(jnp.concatenate([a_s[n], r_s[n]], axis=0).astype(BF16),
                     jnp.concatenate([b_s[n], k_s[n]], axis=0).astype(BF16)) for n in chunks]
    l_p = [jnp.where(strict, m_all[n][0:R, 0:R], 0.0) for n in chunks]
    m_ak = [jnp.where(strict, m_all[n][0:R, R:], 0.0).astype(BF16) for n in chunks]
    m_rb = [jnp.where(incl, m_all[n][R:, 0:R], 0.0).astype(BF16) for n in chunks]
    m_rk = [jnp.where(incl, m_all[n][R:, R:], 0.0).astype(BF16) for n in chunks]
    t_m = [jnp.where(eye, 1.0, 0.0) + l_p[n] for n in chunks]
    for _ in range(int(math.log2(C)) - 1):
        l_pb = [l_p[n].astype(BF16) for n in chunks]
        l_p = [_dot(l_pb[n], l_pb[n]) for n in chunks]
        t_m = [t_m[n] + _dot(t_m[n].astype(BF16), l_p[n].astype(BF16)) for n in chunks]
    t_b = [t_m[n].astype(BF16) for n in chunks]
    mv = [_dot(m_ak[n], v_sb[n]).astype(BF16) for n in chunks]
    w_sb = [_dot(t_b[n], a_s[n].astype(BF16)).astype(BF16) for n in chunks]
    u0_b = [_dot(t_b[n], mv[n]).astype(BF16) for n in chunks]
    bh = [(b_s[n] * pc[n]).astype(BF16) for n in chunks]
    kh = [(k_s[n] * pc[n]).astype(BF16) for n in chunks]
    g_m = [(jnp.where(eye, jnp.broadcast_to(pc[n], (R, R)), 0.0) + _dot_tn(bh[n], w_sb[n])).astype(BF16)
           for n in chunks]
    d_m = [_dot_tn(bh[n], u0_b[n]) + _dot_tn(kh[n], v_sb[n]) for n in chunks]
    q_m = [(r_s[n] + _dot(m_rb[n], w_sb[n])).astype(BF16) for n in chunks]
    y0 = [_dot(m_rb[n], u0_b[n]) + _dot(m_rk[n], v_sb[n]) for n in chunks]
    state = h_ref[...]
    ys = []
    for n in chunks:
        h_b = state.astype(BF16)
        y_s = _dot(q_m[n], h_b) + y0[n]
        state = _dot(g_m[n], h_b) + d_m[n]
        ys.append(y_s[0:C] + y_s[C:2 * C] + y_s[2 * C:3 * C] + y_s[3 * C:4 * C])
    h_ref[...] = state
    y = jnp.concatenate(ys, axis=0)

    mean = _head_mean(y)
    yc = y - mean
    var = _head_mean(yc * yc)
    o = yc * lax.rsqrt(var + RWKV_LN_EPS) * lnw_ref[...] + lnb_ref[...] + bonus
    o_ref[0] = (o * gate).astype(o_ref.dtype)


def _rwkv(p, mu, w0, w2p, a0, a2p, g2p, k_k, k_a, r_k, ln_w, ln_b):
    b, s, _ = p.shape
    C = RWKV_CHUNK * RWKV_CHUNKS_PER_STEP
    full = lambda shape: pl.BlockSpec(shape, lambda bi, c: (0,) * len(shape))
    vec = full((1, GROUP))
    mat = full((128, GROUP))
    return pl.pallas_call(
        _rwkv_kernel,
        grid=(b, s // C),
        in_specs=[pl.BlockSpec((1, C, RWKV_PROJ), lambda bi, c: (bi, c, 0)), full((1, RWKV_PROJ)),
                  vec, mat, vec, mat, mat, vec, vec, vec, vec, vec],
        out_specs=pl.BlockSpec((1, C, GROUP), lambda bi, c: (bi, c, 0)),
        out_shape=jax.ShapeDtypeStruct((b, s, GROUP), BF16),
        scratch_shapes=[pltpu.VMEM((8, RWKV_PROJ), F32), pltpu.VMEM((GROUP, GROUP), F32)],
        compiler_params=_params("arbitrary", "arbitrary"),
        name="rwkv7_mix",
    )(p, mu, w0, w2p, a0, a2p, g2p, k_k, k_a, r_k, ln_w, ln_b)


def _out_proj_kernel(x_ref, y0_ref, y1_ref, y2_ref, y3_ref, w_ref, g_ref, o_ref):
    acc = _dot(y0_ref[...], w_ref[0:GROUP, :])
    acc = acc + _dot(y1_ref[...], w_ref[GROUP:2 * GROUP, :])
    acc = acc + _dot(y2_ref[...], w_ref[2 * GROUP:3 * GROUP, :])
    acc = acc + _dot(y3_ref[...], w_ref[3 * GROUP:4 * GROUP, :])
    ms = jnp.mean(acc * acc, axis=-1, keepdims=True)
    o_ref[...] = x_ref[...] + acc * lax.rsqrt(ms + NORM_EPS) * g_ref[...]


def _out_proj(x2, ys, w, g, tm=512):
    t = x2.shape[0]
    rows = lambda n: pl.BlockSpec((tm, n), lambda i: (i, 0))
    full = lambda shape: pl.BlockSpec(shape, lambda i: (0, 0))
    return pl.pallas_call(
        _out_proj_kernel,
        grid=(t // tm,),
        in_specs=[rows(D_MODEL)] + [rows(GROUP)] * 4 + [full((D_MODEL, D_MODEL)), full((1, D_MODEL))],
        out_specs=rows(D_MODEL),
        out_shape=jax.ShapeDtypeStruct((t, D_MODEL), F32),
        compiler_params=_params("parallel"),
        name="out_proj",
    )(x2, *ys, w, g)


def _ffn_kernel(x_ref, g1_ref, wg_ref, wu_ref, cw_ref, cb_ref, wd_ref, g2_ref, o_ref, tail_ref):
    tm = x_ref.shape[1]
    j = pl.program_id(1)

    @pl.when(j == 0)
    def _():
        tail_ref[...] = jnp.zeros(tail_ref.shape, F32)

    x = x_ref[0]
    ms = jnp.mean(x * x, axis=-1, keepdims=True)
    h = (x * lax.rsqrt(ms + NORM_EPS) * g1_ref[...]).astype(BF16)
    row8 = _iota((8, 1), 0)
    acc = jnp.zeros((tm, D_MODEL), F32)
    for c in range(D_FF // FFN_CHUNK):
        cs = slice(c * FFN_CHUNK, (c + 1) * FFN_CHUNK)
        gate = _dot(h, wg_ref[:, cs])
        up = _dot(h, wu_ref[:, cs])
        tail = tail_ref[:, cs]
        conv = gate * cw_ref[FFN_CONV - 1:FFN_CONV, cs] + cb_ref[:, cs]
        for d in range(1, FFN_CONV):
            rolled = pltpu.roll(gate, d, 0)
            head = jnp.where(row8 < d, pltpu.roll(tail, d, 0), rolled[0:8])
            shifted = jnp.concatenate([head, rolled[8:]], axis=0)
            conv = conv + shifted * cw_ref[FFN_CONV - 1 - d:FFN_CONV - d, cs]
        tail_ref[:, cs] = gate[tm - 8:tm, :]
        inner = 0.7978845608028654 * (conv + 0.044715 * conv * conv * conv)
        act = 0.5 * conv * (1.0 + jnp.tanh(inner))
        acc = acc + _dot((act * up).astype(BF16), wd_ref[cs, :])
    ms2 = jnp.mean(acc * acc, axis=-1, keepdims=True)
    o_ref[0] = x + acc * lax.rsqrt(ms2 + NORM_EPS) * g2_ref[...]


def _ffn(x, g1, wg, wu, cw, cb, wd, g2, tm=512):
    b, s, _ = x.shape
    tm = min(tm, s)
    full = lambda shape: pl.BlockSpec(shape, lambda bi, j: (0, 0), pipeline_mode=pl.Buffered(1))
    blk = pl.BlockSpec((1, tm, D_MODEL), lambda bi, j: (bi, j, 0))
    return pl.pallas_call(
        _ffn_kernel,
        grid=(b, s // tm),
        in_specs=[blk, full((1, D_MODEL)), full((D_MODEL, D_FF)), full((D_MODEL, D_FF)),
                  full((FFN_CONV, D_FF)), full((1, D_FF)), full((D_FF, D_MODEL)), full((1, D_MODEL))],
        out_specs=blk,
        out_shape=jax.ShapeDtypeStruct((b, s, D_MODEL), F32),
        scratch_shapes=[pltpu.VMEM((8, D_FF), F32)],
        compiler_params=_params("arbitrary", "arbitrary"),
        name="conv_glu_ffn",
    )(x, g1, wg, wu, cw, cb, wd, g2)


def _row(v):
    return v.reshape(1, -1).astype(F32)


def _pad_rows(m, start, total=128):
    return jnp.zeros((total, m.shape[1]), m.dtype).at[start:start + m.shape[0]].set(m)


def _head_groups(w):
    d = w.shape[0]
    w4 = w.reshape(d, N_HEADS, HEAD_DIM)
    return jnp.concatenate([w4, jnp.zeros_like(w4)], axis=2).reshape(d, AUG)


def _bias_lane_constants():
    lanes = jnp.arange(AUG)
    head, local = lanes // 128, lanes % 128
    slope = jnp.exp2(-(ALIBI_MAX_EXP / N_HEADS) * (head + 1).astype(F32))
    at = lambda k: (local == AUG_LANE + k).astype(F32)
    zero = jnp.zeros((AUG,), F32)
    fox_q = -(at(0) + at(1) + at(2))
    c_parts = [p.astype(F32) for p in _split3(slope * LOG2E)]
    c_at = lambda base: sum(c_parts[k] * at(base + k) for k in range(3))
    ones_at = lambda base: at(base) + at(base + 1) + at(base + 2)
    rows = jnp.stack([jnp.full((AUG,), DIFF_QK ** -0.5 * LOG2E, F32), -ones_at(0), -ones_at(3),
                      c_at(6) + c_at(9), ones_at(6), ones_at(9), c_at(0) + c_at(3),
                      jnp.full((AUG,), HEAD_DIM ** -0.5 * LOG2E, F32), fox_q] + [zero] * 7)
    src = jnp.arange(N_SMALL)
    place = jnp.stack([((src[:, None] == 4 + head[None, :]) & (local[None, :] == AUG_LANE + k)).astype(BF16)
                       for k in range(3)])
    return rows, place


def kernel(x, norm_mix_pre, norm_mix_post, norm_ffn_pre, norm_ffn_post, w_in, w_out, diff_lambda_q1, diff_lambda_k1, diff_lambda_q2, diff_lambda_k2, diff_subln, ssm_conv_w, ssm_conv_b, ssm_dt_bias, ssm_a_log, ssm_d, ssm_norm, fox_f_bias, fox_norm, rwkv_mu, rwkv_w0, rwkv_w2, rwkv_a0, rwkv_a2, rwkv_g2, rwkv_k_k, rwkv_k_a, rwkv_r_k, rwkv_ln_w, rwkv_ln_b, ffn_w_gate, ffn_w_up, ffn_conv_w, ffn_conv_b, ffn_w_down):
    b, s, d = x.shape
    t = b * s
    depth = w_in.shape[0]
    tq = min(ATT_BLOCK, s)
    nq = s // tq
    o_dt = 768 + 256 + 768
    o_fq = o_dt + 4
    o_ff = o_fq + 768
    o_rp = o_ff + 4
    aug_rows, place = _bias_lane_constants()
    to_qt = lambda u: jnp.swapaxes(u.reshape(b, s, AUG), 1, 2)
    ones = jnp.ones((b, nq, tq, N_HEADS, V_ROWS - HEAD_DIM), BF16)

    def to_vt(u):
        v5 = jnp.concatenate([u.reshape(b, nq, tq, N_HEADS, HEAD_DIM), ones], axis=4)
        return jnp.swapaxes(v5.reshape(b, nq, tq, N_HEADS * V_ROWS), 2, 3)
    for l in range(depth):
        wl = w_in[l]
        w_main = jnp.concatenate(
            [_head_groups(wl[:, 0:256]), _head_groups(wl[:, 256:512]), wl[:, 512:o_dt],
             _head_groups(wl[:, o_fq:o_fq + 256]), _head_groups(wl[:, o_fq + 256:o_fq + 512]),
             wl[:, o_fq + 512:o_ff], wl[:, o_rp:]], axis=1).astype(BF16)
        w_small = jnp.concatenate([wl[:, o_dt:o_fq], wl[:, o_ff:o_rp],
                                   jnp.zeros((d, N_SMALL - 8), F32)], axis=1).astype(BF16)
        fbias = jnp.zeros((1, N_SMALL), F32).at[0, 4:8].set(fox_f_bias[l])
        (qd, kd, vd, u_ssm, qf, kf, vf, u_rwkv, u_small) = _in_proj(
            x.reshape(t, d), _row(norm_mix_pre[l]), w_main, w_small, aug_rows, fbias, place, s, tq)
        u_ssm = u_ssm.reshape(b, s, -1)
        u_rwkv = u_rwkv.reshape(b, s, -1)
        u_small = u_small.reshape(b, s, -1)

        lam_init = 0.8 - 0.6 * math.exp(-0.3 * l)
        g_diff = jnp.zeros((8, GROUP), F32).at[0].set(jnp.tile(diff_subln[l], N_HEADS)).at[1].set(lam_init)
        y_diff = _attention("diff", to_qt(qd), kd.reshape(b, s, AUG), to_vt(vd),
                            [_row(diff_lambda_q1[l]), _row(diff_lambda_k1[l]),
                             _row(diff_lambda_q2[l]), _row(diff_lambda_k2[l])], g_diff, tq)
        g_fox = jnp.zeros((8, GROUP), F32).at[0].set(jnp.tile(fox_norm[l], N_HEADS))
        y_fox = _attention("fox", to_qt(qf), kf.reshape(b, s, AUG), to_vt(vf), [], g_fox, tq)

        pad4 = lambda v: jnp.zeros((1, N_SMALL), F32).at[0, 0:4].set(v)
        y_ssm = _ssd(u_ssm, u_small, ssm_conv_w[l], _row(ssm_conv_b[l]), pad4(ssm_dt_bias[l]),
                     pad4(-jnp.exp(ssm_a_log[l])), _row(jnp.repeat(ssm_d[l], HEAD_DIM)), _row(ssm_norm[l]))

        y_rwkv = _rwkv(u_rwkv, _row(rwkv_mu[l]), _row(rwkv_w0[l]),
                       _pad_rows(rwkv_w2[l], 0).astype(BF16), _row(rwkv_a0[l]),
                       _pad_rows(rwkv_a2[l], 32).astype(BF16), _pad_rows(rwkv_g2[l], 64).astype(BF16),
                       _row(rwkv_k_k[l]), _row(rwkv_k_a[l]), _row(rwkv_r_k[l]),
                       _row(rwkv_ln_w[l]), _row(rwkv_ln_b[l]))

        flat = lambda y: y.reshape(t, GROUP)
        x = _out_proj(x.reshape(t, d), [flat(y_diff), flat(y_ssm), flat(y_fox), flat(y_rwkv)],
                      w_out[l].astype(BF16), _row(norm_mix_post[l])).reshape(b, s, d)

        x = _ffn(x, _row(norm_ffn_pre[l]), ffn_w_gate[l].astype(BF16), ffn_w_up[l].astype(BF16),
                 ffn_conv_w[l], _row(ffn_conv_b[l]), ffn_w_down[l].astype(BF16), _row(norm_ffn_post[l]))
    return x
```

```python
import functools
import math

import jax
import jax.numpy as jnp
from jax import lax
from jax.experimental import pallas as pl
from jax.experimental.pallas import tpu as pltpu

F32 = jnp.float32
BF16 = jnp.bfloat16

D_MODEL = 1024
GROUP = 256
N_HEADS = 4
HEAD_DIM = 64
DIFF_QK = 32
NORM_EPS = 1e-6
ALIBI_MAX_EXP = 8.0
SSM_STATE = 128
SSM_CONV = 4
SSM_CHUNK = 128
RWKV_CHUNK = 64
RWKV_CHUNKS_PER_STEP = 4
RWKV_PROJ = 896
RWKV_LN_EPS = 64e-5
D_FF = 2816
FFN_CONV = 3
FFN_CHUNK = 1408
AUG = 2 * GROUP
AUG_LANE = HEAD_DIM
V_ROWS = HEAD_DIM + 16
LOG2E = 1.4426950408889634
COLS = (0, AUG, 2 * AUG, 2 * AUG + GROUP, 2 * AUG + 5 * GROUP, 3 * AUG + 5 * GROUP, 4 * AUG + 5 * GROUP,
        4 * AUG + 6 * GROUP, 4 * AUG + 6 * GROUP + RWKV_PROJ)
N_MAIN = COLS[-1]
N_SMALL = 128
ATT_BLOCK = 512
ATT_UNIT_WIDTH = 512
VMEM_LIMIT = 56 * 1024 * 1024


def _dot(a, b):
    return jnp.dot(a, b, preferred_element_type=F32)


def _dot_nt(a, b):
    return lax.dot_general(a, b, (((1,), (1,)), ((), ())), preferred_element_type=F32)


def _dot_tn(a, b):
    return lax.dot_general(a, b, (((0,), (0,)), ((), ())), preferred_element_type=F32)


def _split3(x):
    h1 = x.astype(BF16)
    r1 = x - h1.astype(F32)
    h2 = r1.astype(BF16)
    h3 = (r1 - h2.astype(F32)).astype(BF16)
    return h1, h2, h3


def _dot3_left(m_bf16, x):
    h1, h2, h3 = _split3(x)
    return _dot(m_bf16, h1) + _dot(m_bf16, h2) + _dot(m_bf16, h3)


def _dot3_right(x, m_bf16):
    h1, h2, h3 = _split3(x)
    return _dot(h1, m_bf16) + _dot(h2, m_bf16) + _dot(h3, m_bf16)


def _iota(shape, dim):
    return lax.broadcasted_iota(jnp.int32, shape, dim)


def _div(x, n):
    return lax.shift_right_logical(x, int(math.log2(n)))


def _mod(x, n):
    return lax.bitwise_and(x, n - 1)


def _softplus(x):
    return jnp.maximum(x, 0.0) + jnp.log(1.0 + jnp.exp(-jnp.abs(x)))


def _sigmoid(x):
    return 1.0 / (1.0 + jnp.exp(-x))


def _silu(x):
    return x * _sigmoid(x)


def _params(*sem):
    return pltpu.CompilerParams(dimension_semantics=sem, vmem_limit_bytes=VMEM_LIMIT)


def _in_proj_kernel(x_ref, g_ref, w_ref, ws_ref, aug_ref, fb_ref, place_ref,
                    qd_ref, kd_ref, vd_ref, os_ref, qf_ref, kf_ref, vf_ref, or_ref, osm_ref,
                    carry_ref, *, tm, tq, blocks_per_seq):
    i = pl.program_id(0)

    @pl.when(i % blocks_per_seq == 0)
    def _():
        carry_ref[...] = jnp.zeros(carry_ref.shape, F32)

    x = x_ref[...]
    ms = jnp.mean(x * x, axis=-1, keepdims=True)
    h = (x * lax.rsqrt(ms + NORM_EPS) * g_ref[...]).astype(BF16)
    proj = lambda n: _dot(h, w_ref[:, COLS[n]:COLS[n + 1]])
    pos = _mod(_iota((tm, 1), 0), tq)
    lo = _mod(pos, 256).astype(F32)
    hi = pos.astype(F32) - lo
    qd_ref[...] = (proj(0) * aug_ref[0:1, :] + lo * aug_ref[1:2, :] + hi * aug_ref[2:3, :]
                   + aug_ref[3:4, :]).astype(BF16)
    kd_ref[...] = (proj(1) + lo * aug_ref[4:5, :] + hi * aug_ref[5:6, :] + aug_ref[6:7, :]).astype(BF16)
    vd_ref[...] = proj(2).astype(BF16)
    os_ref[...] = proj(3).astype(BF16)
    small = _dot(h, ws_ref[...])
    osm_ref[...] = small
    f = small + fb_ref[...]
    ls = jnp.minimum(f, 0.0) - jnp.log(1.0 + jnp.exp(-jnp.abs(f)))
    tri = (_iota((tm, tm), 0) >= _iota((tm, tm), 1)).astype(BF16)
    cum = _dot3_left(tri, ls) + carry_ref[0:1, :]
    carry_ref[0:1, :] = cum[tm - 1:tm, :]
    c1, c2, c3 = _split3(cum * LOG2E)
    kbias = _dot(c1, place_ref[0]) + _dot(c2, place_ref[1]) + _dot(c3, place_ref[2])
    qf_ref[...] = (proj(4) * aug_ref[7:8, :] + aug_ref[8:9, :]).astype(BF16)
    kf_ref[...] = (proj(5) + kbias).astype(BF16)
    vf_ref[...] = proj(6).astype(BF16)
    or_ref[...] = proj(7).astype(BF16)


def _in_proj(x2, g, w_main, w_small, aug_rows, fbias, place, seq, tq, tm=512):
    t = x2.shape[0]
    tm = min(tm, seq)
    full = lambda shape: pl.BlockSpec(shape, lambda i: (0,) * len(shape))
    rows = lambda n: pl.BlockSpec((tm, n), lambda i: (i, 0))
    widths = (AUG, AUG, GROUP, 4 * GROUP, AUG, AUG, GROUP, RWKV_PROJ)
    return pl.pallas_call(
        functools.partial(_in_proj_kernel, tm=tm, tq=tq, blocks_per_seq=seq // tm),
        grid=(t // tm,),
        in_specs=[rows(D_MODEL), full((1, D_MODEL)), full((D_MODEL, N_MAIN)), full((D_MODEL, N_SMALL)),
                  full((16, AUG)), full((1, N_SMALL)), full((3, N_SMALL, AUG))],
        out_specs=[rows(n) for n in widths] + [rows(N_SMALL)],
        out_shape=[jax.ShapeDtypeStruct((t, n), BF16) for n in widths]
        + [jax.ShapeDtypeStruct((t, N_SMALL), F32)],
        scratch_shapes=[pltpu.VMEM((8, N_SMALL), F32)],
        compiler_params=_params("arbitrary"),
        name="in_proj",
    )(x2, g, w_main, w_small, aug_rows, fbias, place)


def _head_mean_matrix():
    same = _div(_iota((GROUP, GROUP), 0), HEAD_DIM) == _div(_iota((GROUP, GROUP), 1), HEAD_DIM)
    return jnp.where(same, 1.0 / HEAD_DIM, 0.0).astype(BF16)


def _head_mean(x):
    avg = _head_mean_matrix()
    h1 = x.astype(BF16)
    h2 = (x - h1.astype(F32)).astype(BF16)
    return _dot(h1, avg) + _dot(h2, avg)


def _attn_kernel(*refs, kind, tq):
    if kind == "diff":
        (qt_ref, k_ref, vt_ref, lq1_ref, lk1_ref, lq2_ref, lk2_ref, g_ref, o_ref,
         qs_ref, m_ref, acc_ref) = refs
        n_c = 2
    else:
        qt_ref, k_ref, vt_ref, g_ref, o_ref, qs_ref, m_ref, acc_ref = refs
        n_c = 1
    tk = tq
    i = pl.program_id(1)

    sub = _iota((128, 1), 0)
    for h in range(N_HEADS):
        qh = qt_ref[0, h * 128:(h + 1) * 128, :]
        if kind == "diff":
            zero = jnp.zeros_like(qh)
            qs_ref[h, :, 0:tq] = jnp.where((sub >= DIFF_QK) & (sub < 2 * DIFF_QK), zero, qh)
            qs_ref[h, :, tq:2 * tq] = jnp.where(sub < DIFF_QK, zero, qh)
        else:
            qs_ref[h] = qh
    m_ref[...] = jnp.full(m_ref.shape, -jnp.inf, F32)
    acc_ref[...] = jnp.zeros(acc_ref.shape, F32)
    wu = min(ATT_UNIT_WIDTH, tq)
    units = [(h, l0) for h in range(N_HEADS) for l0 in range(0, n_c * tq, wu)]
    kpos = _iota((tk, wu), 0)
    qcol = _iota((tk, wu), 1)

    def run(kv_steps):
        work = [(j, masked, h, l0) for j, masked in kv_steps for h, l0 in units]

        def logits(unit):
            j, _, h, l0 = unit
            start = pl.multiple_of(j * tk, tk)
            return _dot(k_ref[0, pl.ds(start, tk), h * 128:(h + 1) * 128], qs_ref[h, :, l0:l0 + wu])

        def stats(unit, s):
            j, masked, h, l0 = unit
            slope2 = LOG2E * 2.0 ** (-(ALIBI_MAX_EXP / N_HEADS) * (h + 1))
            off = -slope2 * ((i - j) * tq).astype(F32) if kind == "diff" else 0.0
            if masked:
                s = jnp.where(kpos <= qcol + (l0 % tq), s, -jnp.inf)
            m_old = m_ref[h, :, l0:l0 + wu]
            m_new = jnp.maximum(m_old, jnp.max(s, axis=0, keepdims=True) + off)
            m_ref[h, :, l0:l0 + wu] = m_new
            return s, jnp.exp2(m_old - m_new), m_new - off

        def accumulate(unit, s, alpha, shift):
            j, _, h, l0 = unit
            p = jnp.exp2(s - shift).astype(BF16)
            v_h = vt_ref[0, j, h * V_ROWS:(h + 1) * V_ROWS, :]
            acc_ref[h, :, l0:l0 + wu] = acc_ref[h, :, l0:l0 + wu] * alpha + _dot(v_h, p)

        n_work = len(work)
        s_of = {0: logits(work[0])}
        if n_work > 1:
            s_of[1] = logits(work[1])
        st_of = {0: stats(work[0], s_of.pop(0))}
        for n in range(n_work):
            if n + 2 < n_work:
                s_of[n + 2] = logits(work[n + 2])
            if n + 1 < n_work:
                st_of[n + 1] = stats(work[n + 1], s_of.pop(n + 1))
            accumulate(work[n], *st_of.pop(n))

    def body(t, carry):
        run([(2 * t, False), (2 * t + 1, False)])
        return carry

    lax.fori_loop(0, i // 2, body, 0)

    @pl.when(i % 2 == 1)
    def _():
        run([(i - 1, False), (i, True)])

    @pl.when(i % 2 == 0)
    def _():
        run([(i, True)])

    parts = []
    if kind == "diff":
        lam_init = g_ref[1:2, 0:1]
        lam = (jnp.exp(jnp.sum(lq1_ref[...] * lk1_ref[...], axis=1, keepdims=True))
               - jnp.exp(jnp.sum(lq2_ref[...] * lk2_ref[...], axis=1, keepdims=True)) + lam_init)
        for h in range(N_HEADS):
            on = acc_ref[h, 0:HEAD_DIM, :] * (1.0 / acc_ref[h, HEAD_DIM:HEAD_DIM + 1, :])
            parts.append(on[:, 0:tq] - lam * on[:, tq:2 * tq])
    else:
        for h in range(N_HEADS):
            parts.append(acc_ref[h, 0:HEAD_DIM, :] * (1.0 / acc_ref[h, HEAD_DIM:HEAD_DIM + 1, :]))
    o = jnp.concatenate(parts, axis=0).T
    ms = _head_mean(o * o)
    y = o * lax.rsqrt(ms + NORM_EPS) * g_ref[0:1, :]
    if kind == "diff":
        y = y * (1.0 - lam_init)
    o_ref[0] = y.astype(o_ref.dtype)


def _attention(kind, q_t, k_aug, v_t, extra, gains, tq):
    b, s, _ = k_aug.shape
    nq = s // tq
    w = (2 if kind == "diff" else 1) * tq
    extra_specs = [pl.BlockSpec((1, DIFF_QK), lambda bi, i: (0, 0))] * len(extra)
    once = pl.Buffered(1)
    return pl.pallas_call(
        functools.partial(_attn_kernel, kind=kind, tq=tq),
        grid=(b, nq),
        in_specs=[pl.BlockSpec((1, AUG, tq), lambda bi, i: (bi, 0, i)),
                  pl.BlockSpec((1, s, AUG), lambda bi, i: (bi, 0, 0), pipeline_mode=once),
                  pl.BlockSpec((1, nq, N_HEADS * V_ROWS, tq), lambda bi, i: (bi, 0, 0, 0), pipeline_mode=once)]
        + extra_specs + [pl.BlockSpec((8, GROUP), lambda bi, i: (0, 0))],
        out_specs=pl.BlockSpec((1, tq, GROUP), lambda bi, i: (bi, i, 0)),
        out_shape=jax.ShapeDtypeStruct((b, s, GROUP), BF16),
        scratch_shapes=[pltpu.VMEM((N_HEADS, 128, w), BF16), pltpu.VMEM((N_HEADS, 1, w), F32),
                        pltpu.VMEM((N_HEADS, V_ROWS, w), F32)],
        compiler_params=_params("arbitrary", "arbitrary"),
        name=kind + "_attention",
    )(q_t, k_aug, v_t, *extra, gains)


def _ssd_kernel(u_ref, sm_ref, cw_ref, cb_ref, dtb_ref, a_ref, d_ref, g_ref, o_ref,
                prev_ref, state_ref):
    L = SSM_CHUNK
    c = pl.program_id(1)

    @pl.when(c == 0)
    def _():
        prev_ref[...] = jnp.zeros(prev_ref.shape, F32)
        state_ref[...] = jnp.zeros(state_ref.shape, F32)

    u = u_ref[0].astype(F32)
    z = u[:, 0:GROUP]
    raw = u[:, GROUP:]
    prev = prev_ref[...]
    row = _iota((L, 1), 0)
    xbc = raw * cw_ref[SSM_CONV - 1:SSM_CONV, :] + cb_ref[...]
    for d in range(1, SSM_CONV):
        shifted = jnp.where(row >= d, pltpu.roll(raw, d, 0), pltpu.roll(prev, d, 0))
        xbc = xbc + shifted * cw_ref[SSM_CONV - 1 - d:SSM_CONV - d, :]
    prev_ref[...] = raw
    xbc = _silu(xbc)
    xs = xbc[:, 0:GROUP]
    bm = xbc[:, GROUP:2 * GROUP]
    cm = xbc[:, 2 * GROUP:3 * GROUP]

    dt = _softplus(sm_ref[0] + dtb_ref[...])
    da = dt * a_ref[...]
    rep = (_iota((128, 4 * 128), 0) == _div(_iota((128, 4 * 128), 1), 128)).astype(BF16)
    exp_h = (_iota((128, GROUP), 0) == _div(_iota((128, GROUP), 1), HEAD_DIM)).astype(BF16)
    dt_e = _dot3_right(dt, exp_h)
    da_rep = _dot3_right(da, rep)
    tri = (_iota((L, L), 0) >= _iota((L, L), 1)).astype(BF16)
    acs = _dot3_left(tri, da_rep)
    last = acs[L - 1:L, :]
    causal = _iota((L, L), 0) >= _iota((L, L), 1)

    xdt = (xs * dt_e).astype(BF16)
    scores, bd, cd = [], [], []
    for h in range(N_HEADS):
        g = h // 2
        col = acs[:, h * 128:(h + 1) * 128]
        b_g = bm[:, g * 128:(g + 1) * 128]
        c_g = cm[:, g * 128:(g + 1) * 128]
        if h % 2 == 0:
            cb = _dot_nt(c_g.astype(BF16), b_g.astype(BF16))
        decay = jnp.exp(jnp.where(causal, col - col.T, -jnp.inf))
        scores.append((cb * decay).astype(BF16))
        bd.append((b_g * jnp.exp(last[:, h * 128:(h + 1) * 128] - col)).astype(BF16))
        cd.append((c_g * jnp.exp(col)).astype(BF16))
    y_st = _dot(jnp.concatenate(scores, axis=0), xdt)
    hl = _div(_iota((1, GROUP), 1), HEAD_DIM)
    y = jnp.zeros((L, GROUP), F32)
    for h in range(N_HEADS):
        y = y + jnp.where(hl == h, y_st[h * L:(h + 1) * L, :], 0.0)
    state = state_ref[...]
    y = y + _dot(jnp.concatenate(cd, axis=1), state.astype(BF16))
    st_new = _dot_tn(jnp.concatenate(bd, axis=1), xdt)
    blk = _div(_iota((4 * 128, GROUP), 0), 128) == _div(_iota((4 * 128, GROUP), 1), HEAD_DIM)
    chunk_decay = jnp.concatenate(
        [jnp.broadcast_to(jnp.exp(last[:, h * 128:h * 128 + 1]), (128, GROUP)) for h in range(N_HEADS)], axis=0)
    state_ref[...] = state * chunk_decay + jnp.where(blk, st_new, 0.0)

    y = y + xs * d_ref[...]
    yg = y * _silu(z)
    sq = yg * yg
    half = GROUP // 2
    ms0 = jnp.mean(sq[:, 0:half], axis=1, keepdims=True)
    ms1 = jnp.mean(sq[:, half:], axis=1, keepdims=True)
    ms = jnp.where(_iota((1, GROUP), 1) < half, ms0, ms1)
    o_ref[0] = (yg * lax.rsqrt(ms + NORM_EPS) * g_ref[...]).astype(o_ref.dtype)


def _ssd(u, small, conv_w, conv_b, dt_bias, a_neg, d_e, norm_g):
    b, s, _ = u.shape
    L = SSM_CHUNK
    cdim = 3 * GROUP
    full = lambda shape: pl.BlockSpec(shape, lambda bi, c: (0,) * len(shape))
    return pl.pallas_call(
        _ssd_kernel,
        grid=(b, s // L),
        in_specs=[pl.BlockSpec((1, L, 4 * GROUP), lambda bi, c: (bi, c, 0)),
                  pl.BlockSpec((1, L, N_SMALL), lambda bi, c: (bi, c, 0)),
                  full((SSM_CONV, cdim)), full((1, cdim)), full((1, N_SMALL)), full((1, N_SMALL)),
                  full((1, GROUP)), full((1, GROUP))],
        out_specs=pl.BlockSpec((1, L, GROUP), lambda bi, c: (bi, c, 0)),
        out_shape=jax.ShapeDtypeStruct((b, s, GROUP), BF16),
        scratch_shapes=[pltpu.VMEM((L, cdim), F32), pltpu.VMEM((4 * SSM_STATE, GROUP), F32)],
        compiler_params=_params("arbitrary", "arbitrary"),
        name="ssd_mixer",
    )(u, small, conv_w, conv_b, dt_bias, a_neg, d_e, norm_g)


def _rwkv_kernel(p_ref, mu_ref, w0_ref, w2_ref, a0_ref, a2_ref, g2_ref, kk_ref, ka_ref, rk_ref,
                 lnw_ref, lnb_ref, o_ref, last_ref, h_ref):
    C = RWKV_CHUNK
    R = N_HEADS * C
    T = p_ref.shape[1]
    n_chunks = T // C
    c = pl.program_id(1)

    @pl.when(c == 0)
    def _():
        last_ref[...] = jnp.zeros(last_ref.shape, F32)
        h_ref[...] = jnp.zeros(h_ref.shape, F32)

    p = p_ref[0].astype(F32)
    row = _iota((T, 1), 0)
    p_prev = jnp.where(row >= 1, pltpu.roll(p, 1, 0), jnp.broadcast_to(last_ref[0:1, :], p.shape))
    last_ref[0:1, :] = p[T - 1:T, :]
    pm = p + (p_prev - p) * mu_ref[...]
    r = pm[:, 0:GROUP]
    k = pm[:, GROUP:2 * GROUP]
    v = pm[:, 2 * GROUP:3 * GROUP]
    xlow = pm[:, 3 * GROUP:]
    w_raw = -_softplus(-(w0_ref[...] + _dot(jnp.tanh(xlow).astype(BF16), w2_ref[...]))) - 0.5
    logw = -jnp.exp(w_raw)
    a = _sigmoid(a0_ref[...] + _dot(xlow.astype(BF16), a2_ref[...]))
    gate = _dot(_sigmoid(xlow).astype(BF16), g2_ref[...])
    kk = k * kk_ref[...]
    kk = kk * lax.rsqrt(_head_mean(kk * kk) * HEAD_DIM + 1e-12)
    k = k * (1.0 + (a - 1.0) * ka_ref[...])
    bonus = (_head_mean(r * k * rk_ref[...]) * HEAD_DIM) * v

    ti = _iota((T, T), 0)
    tj = _iota((T, T), 1)
    tri = ((ti >= tj) & (_div(ti, C) == _div(tj, C))).astype(BF16)
    lc = _dot3_left(tri, logw)
    e_pos = jnp.exp(lc)
    e_neg = jnp.exp(-lc)
    a_t = -kk * jnp.exp(lc - logw)
    b_t = (a * kk) * e_neg
    k_t = k * e_neg
    r_t = r * e_pos

    hl = _div(_iota((1, GROUP), 1), HEAD_DIM)
    ri = _iota((R, R), 0)
    ci = _iota((R, R), 1)
    same = _div(ri, C) == _div(ci, C)
    strict = same & (_mod(ri, C) > _mod(ci, C))
    incl = same & (_mod(ri, C) >= _mod(ci, C))
    eye = ri == ci
    chunks = range(n_chunks)

    def stack(x, n):
        xc = x[n * C:(n + 1) * C]
        return jnp.concatenate([jnp.where(hl == h, xc, 0.0) for h in range(N_HEADS)], axis=0)

    a_s = [stack(a_t, n) for n in chunks]
    b_s = [stack(b_t, n) for n in chunks]
    k_s = [stack(k_t, n) for n in chunks]
    r_s = [stack(r_t, n) for n in chunks]
    v_sb = [stack(v, n).astype(BF16) for n in chunks]
    pc = [e_pos[n * C + C - 1:n * C + C, :] for n in chunks]
    m_all = [_dot_nt(jnp.concatenate([a_s[n], r_s[n]], axis=0).astype(BF16),
                     jnp.concatenate([b_s[n], k_s[n]], axis=0).astype(BF16)) for n in chunks]
    l_p = [jnp.where(strict, m_all[n][0:R, 0:R], 0.0) for n in chunks]
    m_ak = [jnp.where(strict, m_all[n][0:R, R:], 0.0).astype(BF16) for n in chunks]
    m_rb = [jnp.where(incl, m_all[n][R:, 0:R], 0.0).astype(BF16) for n in chunks]
    m_rk = [jnp.where(incl, m_all[n][R:, R:], 0.0).astype(BF16) for n in chunks]
    t_m = [jnp.where(eye, 1.0, 0.0) + l_p[n] for n in chunks]
    for _ in range(int(math.log2(C)) - 1):
        l_pb = [l_p[n].astype(BF16) for n in chunks]
        l_p = [_dot(l_pb[n], l_pb[n]) for n in chunks]
        t_m = [t_m[n] + _dot(t_m[n].astype(BF16), l_p[n].astype(BF16)) for n in chunks]
    t_b = [t_m[n].astype(BF16) for n in chunks]
    mv = [_dot(m_ak[n], v_sb[n]).astype(BF16) for n in chunks]
    w_sb = [_dot(t_b[n], a_s[n].astype(BF16)).astype(BF16) for n in chunks]
    u0_b = [_dot(t_b[n], mv[n]).astype(BF16) for n in chunks]
    bh = [(b_s[n] * pc[n]).astype(BF16) for n in chunks]
    kh = [(k_s[n] * pc[n]).astype(BF16) for n in chunks]
    g_m = [(jnp.where(eye, jnp.broadcast_to(pc[n], (R, R)), 0.0) + _dot_tn(bh[n], w_sb[n])).astype(BF16)
           for n in chunks]
    d_m = [_dot_tn(bh[n], u0_b[n]) + _dot_tn(kh[n], v_sb[n]) for n in chunks]
    q_m = [(r_s[n] + _dot(m_rb[n], w_sb[n])).astype(BF16) for n in chunks]
    y0 = [_dot(m_rb[n], u0_b[n]) + _dot(m_rk[n], v_sb[n]) for n in chunks]
    state = h_ref[...]
    ys = []
    for n in chunks:
        h_b = state.astype(BF16)
        y_s = _dot(q_m[n], h_b) + y0[n]
        state = _dot(g_m[n], h_b) + d_m[n]
        ys.append(y_s[0:C] + y_s[C:2 * C] + y_s[2 * C:3 * C] + y_s[3 * C:4 * C])
    h_ref[...] = state
    y = jnp.concatenate(ys, axis=0)

    mean = _head_mean(y)
    yc = y - mean
    var = _head_mean(yc * yc)
    o = yc * lax.rsqrt(var + RWKV_LN_EPS) * lnw_ref[...] + lnb_ref[...] + bonus
    o_ref[0] = (o * gate).astype(o_ref.dtype)


def _rwkv(p, mu, w0, w2p, a0, a2p, g2p, k_k, k_a, r_k, ln_w, ln_b):
    b, s, _ = p.shape
    C = RWKV_CHUNK * RWKV_CHUNKS_PER_STEP
    full = lambda shape: pl.BlockSpec(shape, lambda bi, c: (0,) * len(shape))
    vec = full((1, GROUP))
    mat = full((128, GROUP))
    return pl.pallas_call(
        _rwkv_kernel,
        grid=(b, s // C),
        in_specs=[pl.BlockSpec((1, C, RWKV_PROJ), lambda bi, c: (bi, c, 0)), full((1, RWKV_PROJ)),
                  vec, mat, vec, mat, mat, vec, vec, vec, vec, vec],
        out_specs=pl.BlockSpec((1, C, GROUP), lambda bi, c: (bi, c, 0)),
        out_shape=jax.ShapeDtypeStruct((b, s, GROUP), BF16),
        scratch_shapes=[pltpu.VMEM((8, RWKV_PROJ), F32), pltpu.VMEM((GROUP, GROUP), F32)],
        compiler_params=_params("arbitrary", "arbitrary"),
        name="rwkv7_mix",
    )(p, mu, w0, w2p, a0, a2p, g2p, k_k, k_a, r_k, ln_w, ln_b)


def _out_proj_kernel(x_ref, y0_ref, y1_ref, y2_ref, y3_ref, w_ref, g_ref, o_ref):
    acc = _dot(y0_ref[...], w_ref[0:GROUP, :])
    acc = acc + _dot(y1_ref[...], w_ref[GROUP:2 * GROUP, :])
    acc = acc + _dot(y2_ref[...], w_ref[2 * GROUP:3 * GROUP, :])
    acc = acc + _dot(y3_ref[...], w_ref[3 * GROUP:4 * GROUP, :])
    ms = jnp.mean(acc * acc, axis=-1, keepdims=True)
    o_ref[...] = x_ref[...] + acc * lax.rsqrt(ms + NORM_EPS) * g_ref[...]


def _out_proj(x2, ys, w, g, tm=512):
    t = x2.shape[0]
    rows = lambda n: pl.BlockSpec((tm, n), lambda i: (i, 0))
    full = lambda shape: pl.BlockSpec(shape, lambda i: (0, 0))
    return pl.pallas_call(
        _out_proj_kernel,
        grid=(t // tm,),
        in_specs=[rows(D_MODEL)] + [rows(GROUP)] * 4 + [full((D_MODEL, D_MODEL)), full((1, D_MODEL))],
        out_specs=rows(D_MODEL),
        out_shape=jax.ShapeDtypeStruct((t, D_MODEL), F32),
        compiler_params=_params("parallel"),
        name="out_proj",
    )(x2, *ys, w, g)


def _ffn_kernel(x_ref, g1_ref, wg_ref, wu_ref, cw_ref, cb_ref, wd_ref, g2_ref, o_ref, tail_ref):
    tm = x_ref.shape[1]
    j = pl.program_id(1)

    @pl.when(j == 0)
    def _():
        tail_ref[...] = jnp.zeros(tail_ref.shape, F32)

    x = x_ref[0]
    ms = jnp.mean(x * x, axis=-1, keepdims=True)
    h = (x * lax.rsqrt(ms + NORM_EPS) * g1_ref[...]).astype(BF16)
    row8 = _iota((8, 1), 0)
    acc = jnp.zeros((tm, D_MODEL), F32)
    for c in range(D_FF // FFN_CHUNK):
        cs = slice(c * FFN_CHUNK, (c + 1) * FFN_CHUNK)
        gate = _dot(h, wg_ref[:, cs])
        up = _dot(h, wu_ref[:, cs])
        tail = tail_ref[:, cs]
        conv = gate * cw_ref[FFN_CONV - 1:FFN_CONV, cs] + cb_ref[:, cs]
        for d in range(1, FFN_CONV):
            rolled = pltpu.roll(gate, d, 0)
            head = jnp.where(row8 < d, pltpu.roll(tail, d, 0), rolled[0:8])
            shifted = jnp.concatenate([head, rolled[8:]], axis=0)
            conv = conv + shifted * cw_ref[FFN_CONV - 1 - d:FFN_CONV - d, cs]
        tail_ref[:, cs] = gate[tm - 8:tm, :]
        inner = 0.7978845608028654 * (conv + 0.044715 * conv * conv * conv)
        act = 0.5 * conv * (1.0 + jnp.tanh(inner))
        acc = acc + _dot((act * up).astype(BF16), wd_ref[cs, :])
    ms2 = jnp.mean(acc * acc, axis=-1, keepdims=True)
    o_ref[0] = x + acc * lax.rsqrt(ms2 + NORM_EPS) * g2_ref[...]


def _ffn(x, g1, wg, wu, cw, cb, wd, g2, tm=512):
    b, s, _ = x.shape
    tm = min(tm, s)
    full = lambda shape: pl.BlockSpec(shape, lambda bi, j: (0, 0), pipeline_mode=pl.Buffered(1))
    blk = pl.BlockSpec((1, tm, D_MODEL), lambda bi, j: (bi, j, 0))
    return pl.pallas_call(
        _ffn_kernel,
        grid=(b, s // tm),
        in_specs=[blk, full((1, D_MODEL)), full((D_MODEL, D_FF)), full((D_MODEL, D_FF)),
                  full((FFN_CONV, D_FF)), full((1, D_FF)), full((D_FF, D_MODEL)), full((1, D_MODEL))],
        out_specs=blk,
        out_shape=jax.ShapeDtypeStruct((b, s, D_MODEL), F32),
        scratch_shapes=[pltpu.VMEM((8, D_FF), F32)],
        compiler_params=_params("arbitrary", "arbitrary"),
        name="conv_glu_ffn",
    )(x, g1, wg, wu, cw, cb, wd, g2)


def _row(v):
    return v.reshape(1, -1).astype(F32)


def _pad_rows(m, start, total=128):
    return jnp.zeros((total, m.shape[1]), m.dtype).at[start:start + m.shape[0]].set(m)


def _head_groups(w):
    d = w.shape[0]
    w4 = w.reshape(d, N_HEADS, HEAD_DIM)
    return jnp.concatenate([w4, jnp.zeros_like(w4)], axis=2).reshape(d, AUG)


def _bias_lane_constants():
    lanes = jnp.arange(AUG)
    head, local = lanes // 128, lanes % 128
    slope = jnp.exp2(-(ALIBI_MAX_EXP / N_HEADS) * (head + 1).astype(F32))
    at = lambda k: (local == AUG_LANE + k).astype(F32)
    zero = jnp.zeros((AUG,), F32)
    fox_q = -(at(0) + at(1) + at(2))
    c_parts = [p.astype(F32) for p in _split3(slope * LOG2E)]
    c_at = lambda base: sum(c_parts[k] * at(base + k) for k in range(3))
    ones_at = lambda base: at(base) + at(base + 1) + at(base + 2)
    rows = jnp.stack([jnp.full((AUG,), DIFF_QK ** -0.5 * LOG2E, F32), -ones_at(0), -ones_at(3),
                      c_at(6) + c_at(9), ones_at(6), ones_at(9), c_at(0) + c_at(3),
                      jnp.full((AUG,), HEAD_DIM ** -0.5 * LOG2E, F32), fox_q] + [zero] * 7)
    src = jnp.arange(N_SMALL)
    place = jnp.stack([((src[:, None] == 4 + head[None, :]) & (local[None, :] == AUG_LANE + k)).astype(BF16)
                       for k in range(3)])
    return rows, place


def kernel(x, norm_mix_pre, norm_mix_post, norm_ffn_pre, norm_ffn_post, w_in, w_out, diff_lambda_q1, diff_lambda_k1, diff_lambda_q2, diff_lambda_k2, diff_subln, ssm_conv_w, ssm_conv_b, ssm_dt_bias, ssm_a_log, ssm_d, ssm_norm, fox_f_bias, fox_norm, rwkv_mu, rwkv_w0, rwkv_w2, rwkv_a0, rwkv_a2, rwkv_g2, rwkv_k_k, rwkv_k_a, rwkv_r_k, rwkv_ln_w, rwkv_ln_b, ffn_w_gate, ffn_w_up, ffn_conv_w, ffn_conv_b, ffn_w_down):
    b, s, d = x.shape
    t = b * s
    depth = w_in.shape[0]
    tq = min(ATT_BLOCK, s)
    nq = s // tq
    o_dt = 768 + 256 + 768
    o_fq = o_dt + 4
    o_ff = o_fq + 768
    o_rp = o_ff + 4
    aug_rows, place = _bias_lane_constants()
    to_qt = lambda u: jnp.swapaxes(u.reshape(b, s, AUG), 1, 2)
    ones = jnp.ones((b, nq, tq, N_HEADS, V_ROWS - HEAD_DIM), BF16)

    def to_vt(u):
        v5 = jnp.concatenate([u.reshape(b, nq, tq, N_HEADS, HEAD_DIM), ones], axis=4)
        return jnp.swapaxes(v5.reshape(b, nq, tq, N_HEADS * V_ROWS), 2, 3)
    for l in range(depth):
        wl = w_in[l]
        w_main = jnp.concatenate(
            [_head_groups(wl[:, 0:256]), _head_groups(wl[:, 256:512]), wl[:, 512:o_dt],
             _head_groups(wl[:, o_fq:o_fq + 256]), _head_groups(wl[:, o_fq + 256:o_fq + 512]),
             wl[:, o_fq + 512:o_ff], wl[:, o_rp:]], axis=1).astype(BF16)
        w_small = jnp.concatenate([wl[:, o_dt:o_fq], wl[:, o_ff:o_rp],
                                   jnp.zeros((d, N_SMALL - 8), F32)], axis=1).astype(BF16)
        fbias = jnp.zeros((1, N_SMALL), F32).at[0, 4:8].set(fox_f_bias[l])
        (qd, kd, vd, u_ssm, qf, kf, vf, u_rwkv, u_small) = _in_proj(
            x.reshape(t, d), _row(norm_mix_pre[l]), w_main, w_small, aug_rows, fbias, place, s, tq)
        u_ssm = u_ssm.reshape(b, s, -1)
        u_rwkv = u_rwkv.reshape(b, s, -1)
        u_small = u_small.reshape(b, s, -1)

        lam_init = 0.8 - 0.6 * math.exp(-0.3 * l)
        g_diff = jnp.zeros((8, GROUP), F32).at[0].set(jnp.tile(diff_subln[l], N_HEADS)).at[1].set(lam_init)
        y_diff = _attention("diff", to_qt(qd), kd.reshape(b, s, AUG), to_vt(vd),
                            [_row(diff_lambda_q1[l]), _row(diff_lambda_k1[l]),
                             _row(diff_lambda_q2[l]), _row(diff_lambda_k2[l])], g_diff, tq)
        g_fox = jnp.zeros((8, GROUP), F32).at[0].set(jnp.tile(fox_norm[l], N_HEADS))
        y_fox = _attention("fox", to_qt(qf), kf.reshape(b, s, AUG), to_vt(vf), [], g_fox, tq)

        pad4 = lambda v: jnp.zeros((1, N_SMALL), F32).at[0, 0:4].set(v)
        y_ssm = _ssd(u_ssm, u_small, ssm_conv_w[l], _row(ssm_conv_b[l]), pad4(ssm_dt_bias[l]),
                     pad4(-jnp.exp(ssm_a_log[l])), _row(jnp.repeat(ssm_d[l], HEAD_DIM)), _row(ssm_norm[l]))

        y_rwkv = _rwkv(u_rwkv, _row(rwkv_mu[l]), _row(rwkv_w0[l]),
                       _pad_rows(rwkv_w2[l], 0).astype(BF16), _row(rwkv_a0[l]),
                       _pad_rows(rwkv_a2[l], 32).astype(BF16), _pad_rows(rwkv_g2[l], 64).astype(BF16),
                       _row(rwkv_k_k[l]), _row(rwkv_k_a[l]), _row(rwkv_r_k[l]),
                       _row(rwkv_ln_w[l]), _row(rwkv_ln_b[l]))

        flat = lambda y: y.reshape(t, GROUP)
        x = _out_proj(x.reshape(t, d), [flat(y_diff), flat(y_ssm), flat(y_fox), flat(y_rwkv)],
                      w_out[l].astype(BF16), _row(norm_mix_post[l])).reshape(b, s, d)

        x = _ffn(x, _row(norm_ffn_pre[l]), ffn_w_gate[l].astype(BF16), ffn_w_up[l].astype(BF16),
                 ffn_conv_w[l], _row(ffn_conv_b[l]), ffn_w_down[l].astype(BF16), _row(norm_ffn_post[l]))
    return x
```

```python
import functools
import math

import jax
import jax.numpy as jnp
from jax import lax
from jax.experimental import pallas as pl
from jax.experimental.pallas import tpu as pltpu

F32 = jnp.float32
BF16 = jnp.bfloat16

D_MODEL = 1024
GROUP = 256
N_HEADS = 4
HEAD_DIM = 64
DIFF_QK = 32
NORM_EPS = 1e-6
ALIBI_MAX_EXP = 8.0
SSM_STATE = 128
SSM_CONV = 4
SSM_CHUNK = 128
RWKV_CHUNK = 64
RWKV_CHUNKS_PER_STEP = 4
RWKV_PROJ = 896
RWKV_LN_EPS = 64e-5
D_FF = 2816
FFN_CONV = 3
FFN_CHUNK = 1408
AUG = 2 * GROUP
AUG_LANE = HEAD_DIM
V_ROWS = HEAD_DIM + 16
LOG2E = 1.4426950408889634
TCOLS = (0, AUG, AUG + N_HEADS * V_ROWS, 2 * AUG + N_HEADS * V_ROWS, 2 * AUG + 2 * N_HEADS * V_ROWS)
N_T = TCOLS[-1]
COLS = (0, AUG, AUG + 4 * GROUP, 2 * AUG + 4 * GROUP, 2 * AUG + 4 * GROUP + RWKV_PROJ)
N_MAIN = COLS[-1]
N_SMALL = 128
ATT_BLOCK = 512
ATT_UNIT_WIDTH = 512
VMEM_LIMIT = 56 * 1024 * 1024


def _dot(a, b):
    return jnp.dot(a, b, preferred_element_type=F32)


def _dot_nt(a, b):
    return lax.dot_general(a, b, (((1,), (1,)), ((), ())), preferred_element_type=F32)


def _dot_tn(a, b):
    return lax.dot_general(a, b, (((0,), (0,)), ((), ())), preferred_element_type=F32)


def _split3(x):
    h1 = x.astype(BF16)
    r1 = x - h1.astype(F32)
    h2 = r1.astype(BF16)
    h3 = (r1 - h2.astype(F32)).astype(BF16)
    return h1, h2, h3


def _dot3_left(m_bf16, x):
    h1, h2, h3 = _split3(x)
    return _dot(m_bf16, h1) + _dot(m_bf16, h2) + _dot(m_bf16, h3)


def _dot3_right(x, m_bf16):
    h1, h2, h3 = _split3(x)
    return _dot(h1, m_bf16) + _dot(h2, m_bf16) + _dot(h3, m_bf16)


def _iota(shape, dim):
    return lax.broadcasted_iota(jnp.int32, shape, dim)


def _div(x, n):
    return lax.shift_right_logical(x, int(math.log2(n)))


def _mod(x, n):
    return lax.bitwise_and(x, n - 1)


def _softplus(x):
    return jnp.maximum(x, 0.0) + jnp.log(1.0 + jnp.exp(-jnp.abs(x)))


def _sigmoid(x):
    return 1.0 / (1.0 + jnp.exp(-x))


def _silu(x):
    return x * _sigmoid(x)


def _params(*sem):
    return pltpu.CompilerParams(dimension_semantics=sem, vmem_limit_bytes=VMEM_LIMIT)


def _in_proj_kernel(x_ref, g_ref, w_ref, wt_ref, ws_ref, aug_ref, augt_ref, vones_ref, fb_ref, place_ref,
                    kd_ref, os_ref, kf_ref, or_ref, osm_ref, qdt_ref, vdt_ref, qft_ref, vft_ref,
                    carry_ref, *, tm, tq, blocks_per_seq):
    i = pl.program_id(0)

    @pl.when(i % blocks_per_seq == 0)
    def _():
        carry_ref[...] = jnp.zeros(carry_ref.shape, F32)

    x = x_ref[...]
    ms = jnp.mean(x * x, axis=-1, keepdims=True)
    h = (x * lax.rsqrt(ms + NORM_EPS) * g_ref[...]).astype(BF16)
    proj = lambda n: _dot(h, w_ref[:, COLS[n]:COLS[n + 1]])
    proj_t = lambda n: _dot_nt(wt_ref[TCOLS[n]:TCOLS[n + 1], :], h)
    col = lambda k: augt_ref[:, k:k + 1]
    pos = _mod(_iota((tm, 1), 0), tq)
    lo = _mod(pos, 256).astype(F32)
    hi = pos.astype(F32) - lo
    pos_t = _mod(_iota((1, tm), 1), tq)
    lo_t = _mod(pos_t, 256).astype(F32)
    hi_t = pos_t.astype(F32) - lo_t
    qdt_ref[0] = (proj_t(0) * col(0) + col(1) * lo_t + col(2) * hi_t + col(3)).astype(BF16)
    vdt_ref[0, 0] = (proj_t(1) + vones_ref[...]).astype(BF16)
    qft_ref[0] = (proj_t(2) * col(7) + col(8)).astype(BF16)
    vft_ref[0, 0] = (proj_t(3) + vones_ref[...]).astype(BF16)
    kd_ref[...] = (proj(0) + lo * aug_ref[4:5, :] + hi * aug_ref[5:6, :] + aug_ref[6:7, :]).astype(BF16)
    os_ref[...] = proj(1).astype(BF16)
    small = _dot(h, ws_ref[...])
    osm_ref[...] = small
    f = small + fb_ref[...]
    ls = jnp.minimum(f, 0.0) - jnp.log(1.0 + jnp.exp(-jnp.abs(f)))
    tri = (_iota((tm, tm), 0) >= _iota((tm, tm), 1)).astype(BF16)
    cum = _dot3_left(tri, ls) + carry_ref[0:1, :]
    carry_ref[0:1, :] = cum[tm - 1:tm, :]
    c1, c2, c3 = _split3(cum * LOG2E)
    kbias = _dot(c1, place_ref[0]) + _dot(c2, place_ref[1]) + _dot(c3, place_ref[2])
    kf_ref[...] = (proj(2) + kbias).astype(BF16)
    or_ref[...] = proj(3).astype(BF16)


def _in_proj(x2, g, w_main, w_t, w_small, aug_rows, fbias, place, batch, seq, tq):
    t = x2.shape[0]
    tm = tq
    bps = seq // tm
    nv = N_HEADS * V_ROWS
    full = lambda shape: pl.BlockSpec(shape, lambda i: (0,) * len(shape))
    rows = lambda n: pl.BlockSpec((tm, n), lambda i: (i, 0))
    q_t = pl.BlockSpec((1, AUG, tm), lambda i: (i // bps, 0, i % bps))
    v_t = pl.BlockSpec((1, 1, nv, tm), lambda i: (i // bps, i % bps, 0, 0))
    widths = (AUG, 4 * GROUP, AUG, RWKV_PROJ)
    lanes = jnp.arange(nv) % V_ROWS
    vones = (lanes >= HEAD_DIM).astype(F32).reshape(nv, 1)
    q_shape = jax.ShapeDtypeStruct((batch, AUG, seq), BF16)
    v_shape = jax.ShapeDtypeStruct((batch, bps, nv, tm), BF16)
    return pl.pallas_call(
        functools.partial(_in_proj_kernel, tm=tm, tq=tq, blocks_per_seq=bps),
        grid=(t // tm,),
        in_specs=[rows(D_MODEL), full((1, D_MODEL)), full((D_MODEL, N_MAIN)), full((N_T, D_MODEL)),
                  full((D_MODEL, N_SMALL)), full((16, AUG)), full((AUG, 16)), full((nv, 1)),
                  full((1, N_SMALL)), full((3, N_SMALL, AUG))],
        out_specs=[rows(n) for n in widths] + [rows(N_SMALL), q_t, v_t, q_t, v_t],
        out_shape=[jax.ShapeDtypeStruct((t, n), BF16) for n in widths]
        + [jax.ShapeDtypeStruct((t, N_SMALL), F32), q_shape, v_shape, q_shape, v_shape],
        scratch_shapes=[pltpu.VMEM((8, N_SMALL), F32)],
        compiler_params=_params("arbitrary"),
        name="in_proj",
    )(x2, g, w_main, w_t, w_small, aug_rows, aug_rows.T, vones, fbias, place)


def _head_mean_matrix():
    same = _div(_iota((GROUP, GROUP), 0), HEAD_DIM) == _div(_iota((GROUP, GROUP), 1), HEAD_DIM)
    return jnp.where(same, 1.0 / HEAD_DIM, 0.0).astype(BF16)


def _head_mean(x):
    return _dot(x.astype(BF16), _head_mean_matrix())


def _attn_kernel(*refs, kind, tq):
    if kind == "diff":
        (qt_ref, k_ref, vt_ref, lq1_ref, lk1_ref, lq2_ref, lk2_ref, g_ref, o_ref,
         qs_ref, m_ref, acc_ref) = refs
        n_c = 2
    else:
        qt_ref, k_ref, vt_ref, g_ref, o_ref, qs_ref, m_ref, acc_ref = refs
        n_c = 1
    tk = tq
    i = pl.program_id(1)

    sub = _iota((128, 1), 0)
    for h in range(N_HEADS):
        qh = qt_ref[0, h * 128:(h + 1) * 128, :]
        if kind == "diff":
            zero = jnp.zeros_like(qh)
            qs_ref[h, :, 0:tq] = jnp.where((sub >= DIFF_QK) & (sub < 2 * DIFF_QK), zero, qh)
            qs_ref[h, :, tq:2 * tq] = jnp.where(sub < DIFF_QK, zero, qh)
        else:
            qs_ref[h] = qh
    m_ref[...] = jnp.full(m_ref.shape, -jnp.inf, F32)
    acc_ref[...] = jnp.zeros(acc_ref.shape, F32)
    wu = min(ATT_UNIT_WIDTH, tq)
    units = [(h, l0) for h in range(N_HEADS) for l0 in range(0, n_c * tq, wu)]
    kpos = _iota((tk, wu), 0)
    qcol = _iota((tk, wu), 1)

    def run(kv_steps):
        work = [(j, masked, h, l0) for j, masked in kv_steps for h, l0 in units]

        def logits(unit):
            j, _, h, l0 = unit
            start = pl.multiple_of(j * tk, tk)
            return _dot(k_ref[0, pl.ds(start, tk), h * 128:(h + 1) * 128], qs_ref[h, :, l0:l0 + wu])

        def stats(unit, s):
            j, masked, h, l0 = unit
            slope2 = LOG2E * 2.0 ** (-(ALIBI_MAX_EXP / N_HEADS) * (h + 1))
            off = -slope2 * ((i - j) * tq).astype(F32) if kind == "diff" else 0.0
            if masked:
                s = jnp.where(kpos <= qcol + (l0 % tq), s, -jnp.inf)
            m_old = m_ref[h, :, l0:l0 + wu]
            m_new = jnp.maximum(m_old, jnp.max(s, axis=0, keepdims=True) + off)
            m_ref[h, :, l0:l0 + wu] = m_new
            return s, jnp.exp2(m_old - m_new), m_new - off

        def accumulate(unit, s, alpha, shift):
            j, _, h, l0 = unit
            p = jnp.exp2(s - shift).astype(BF16)
            v_h = vt_ref[0, j, h * V_ROWS:(h + 1) * V_ROWS, :]
            acc_ref[h, :, l0:l0 + wu] = acc_ref[h, :, l0:l0 + wu] * alpha + _dot(v_h, p)

        n_work = len(work)
        s_of = {0: logits(work[0])}
        if n_work > 1:
            s_of[1] = logits(work[1])
        st_of = {0: stats(work[0], s_of.pop(0))}
        for n in range(n_work):
            if n + 2 < n_work:
                s_of[n + 2] = logits(work[n + 2])
            if n + 1 < n_work:
                st_of[n + 1] = stats(work[n + 1], s_of.pop(n + 1))
            accumulate(work[n], *st_of.pop(n))

    def body(t, carry):
        run([(2 * t, False), (2 * t + 1, False)])
        return carry

    lax.fori_loop(0, i // 2, body, 0)

    @pl.when(i % 2 == 1)
    def _():
        run([(i - 1, False), (i, True)])

    @pl.when(i % 2 == 0)
    def _():
        run([(i, True)])

    parts = []
    if kind == "diff":
        lam_init = g_ref[1:2, 0:1]
        lam = (jnp.exp(jnp.sum(lq1_ref[...] * lk1_ref[...], axis=1, keepdims=True))
               - jnp.exp(jnp.sum(lq2_ref[...] * lk2_ref[...], axis=1, keepdims=True)) + lam_init)
        for h in range(N_HEADS):
            on = acc_ref[h, 0:HEAD_DIM, :] * (1.0 / acc_ref[h, HEAD_DIM:HEAD_DIM + 1, :])
            parts.append(on[:, 0:tq] - lam * on[:, tq:2 * tq])
    else:
        for h in range(N_HEADS):
            parts.append(acc_ref[h, 0:HEAD_DIM, :] * (1.0 / acc_ref[h, HEAD_DIM:HEAD_DIM + 1, :]))
    o = jnp.concatenate(parts, axis=0).T
    ms = _head_mean(o * o)
    y = o * lax.rsqrt(ms + NORM_EPS) * g_ref[0:1, :]
    if kind == "diff":
        y = y * (1.0 - lam_init)
    o_ref[0] = y.astype(o_ref.dtype)


def _attention(kind, q_t, k_aug, v_t, extra, gains, tq):
    b, s, _ = k_aug.shape
    nq = s // tq
    w = (2 if kind == "diff" else 1) * tq
    extra_specs = [pl.BlockSpec((1, DIFF_QK), lambda bi, i: (0, 0))] * len(extra)
    once = pl.Buffered(1)
    return pl.pallas_call(
        functools.partial(_attn_kernel, kind=kind, tq=tq),
        grid=(b, nq),
        in_specs=[pl.BlockSpec((1, AUG, tq), lambda bi, i: (bi, 0, i)),
                  pl.BlockSpec((1, s, AUG), lambda bi, i: (bi, 0, 0), pipeline_mode=once),
                  pl.BlockSpec((1, nq, N_HEADS * V_ROWS, tq), lambda bi, i: (bi, 0, 0, 0), pipeline_mode=once)]
        + extra_specs + [pl.BlockSpec((8, GROUP), lambda bi, i: (0, 0))],
        out_specs=pl.BlockSpec((1, tq, GROUP), lambda bi, i: (bi, i, 0)),
        out_shape=jax.ShapeDtypeStruct((b, s, GROUP), BF16),
        scratch_shapes=[pltpu.VMEM((N_HEADS, 128, w), BF16), pltpu.VMEM((N_HEADS, 1, w), F32),
                        pltpu.VMEM((N_HEADS, V_ROWS, w), F32)],
        compiler_params=_params("arbitrary", "arbitrary"),
        name=kind + "_attention",
    )(q_t, k_aug, v_t, *extra, gains)


def _ssd_kernel(u_ref, sm_ref, cw_ref, cb_ref, dtb_ref, a_ref, d_ref, g_ref, o_ref,
                prev_ref, state_ref):
    L = SSM_CHUNK
    c = pl.program_id(1)

    @pl.when(c == 0)
    def _():
        prev_ref[...] = jnp.zeros(prev_ref.shape, F32)
        state_ref[...] = jnp.zeros(state_ref.shape, F32)

    u = u_ref[0].astype(F32)
    z = u[:, 0:GROUP]
    raw = u[:, GROUP:]
    tail = prev_ref[...]
    row8 = _iota((8, 1), 0)
    xbc = raw * cw_ref[SSM_CONV - 1:SSM_CONV, :] + cb_ref[...]
    for d in range(1, SSM_CONV):
        rolled = pltpu.roll(raw, d, 0)
        head = jnp.where(row8 < d, pltpu.roll(tail, d, 0), rolled[0:8])
        shifted = jnp.concatenate([head, rolled[8:]], axis=0)
        xbc = xbc + shifted * cw_ref[SSM_CONV - 1 - d:SSM_CONV - d, :]
    prev_ref[...] = raw[L - 8:L, :]
    xbc = _silu(xbc)
    xs = xbc[:, 0:GROUP]
    bm = xbc[:, GROUP:2 * GROUP]
    cm = xbc[:, 2 * GROUP:3 * GROUP]

    dt = _softplus(sm_ref[0] + dtb_ref[...])
    da = dt * a_ref[...]
    rep = (_iota((128, 4 * 128), 0) == _div(_iota((128, 4 * 128), 1), 128)).astype(BF16)
    exp_h = (_iota((128, GROUP), 0) == _div(_iota((128, GROUP), 1), HEAD_DIM)).astype(BF16)
    dt_e = _dot3_right(dt, exp_h)
    tri = (_iota((L, L), 0) >= _iota((L, L), 1)).astype(BF16)
    acs = _dot3_right(_dot3_left(tri, da), rep)
    last = acs[L - 1:L, :]
    causal = _iota((L, L), 0) >= _iota((L, L), 1)

    xdt = (xs * dt_e).astype(BF16)
    scores, bd, cd = [], [], []
    for h in range(N_HEADS):
        g = h // 2
        col = acs[:, h * 128:(h + 1) * 128]
        b_g = bm[:, g * 128:(g + 1) * 128]
        c_g = cm[:, g * 128:(g + 1) * 128]
        if h % 2 == 0:
            cb = _dot_nt(c_g.astype(BF16), b_g.astype(BF16))
        decay = jnp.exp(jnp.where(causal, col - col.T, -jnp.inf))
        scores.append((cb * decay).astype(BF16))
        bd.append((b_g * jnp.exp(last[:, h * 128:(h + 1) * 128] - col)).astype(BF16))
        cd.append((c_g * jnp.exp(col)).astype(BF16))
    y_st = _dot(jnp.concatenate(scores, axis=0), xdt)
    hl = _div(_iota((1, GROUP), 1), HEAD_DIM)
    y = jnp.zeros((L, GROUP), F32)
    for h in range(N_HEADS):
        y = y + jnp.where(hl == h, y_st[h * L:(h + 1) * L, :], 0.0)
    state = state_ref[...]
    y = y + _dot(jnp.concatenate(cd, axis=1), state.astype(BF16))
    st_new = _dot_tn(jnp.concatenate(bd, axis=1), xdt)
    blk = _div(_iota((4 * 128, GROUP), 0), 128) == _div(_iota((4 * 128, GROUP), 1), HEAD_DIM)
    chunk_decay = jnp.concatenate(
        [jnp.broadcast_to(jnp.exp(last[:, h * 128:h * 128 + 1]), (128, GROUP)) for h in range(N_HEADS)], axis=0)
    state_ref[...] = state * chunk_decay + jnp.where(blk, st_new, 0.0)

    y = y + xs * d_ref[...]
    yg = y * _silu(z)
    sq = yg * yg
    half = GROUP // 2
    ms0 = jnp.mean(sq[:, 0:half], axis=1, keepdims=True)
    ms1 = jnp.mean(sq[:, half:], axis=1, keepdims=True)
    ms = jnp.where(_iota((1, GROUP), 1) < half, ms0, ms1)
    o_ref[0] = (yg * lax.rsqrt(ms + NORM_EPS) * g_ref[...]).astype(o_ref.dtype)


def _ssd(u, small, conv_w, conv_b, dt_bias, a_neg, d_e, norm_g):
    b, s, _ = u.shape
    L = SSM_CHUNK
    cdim = 3 * GROUP
    full = lambda shape: pl.BlockSpec(shape, lambda bi, c: (0,) * len(shape))
    return pl.pallas_call(
        _ssd_kernel,
        grid=(b, s // L),
        in_specs=[pl.BlockSpec((1, L, 4 * GROUP), lambda bi, c: (bi, c, 0)),
                  pl.BlockSpec((1, L, N_SMALL), lambda bi, c: (bi, c, 0)),
                  full((SSM_CONV, cdim)), full((1, cdim)), full((1, N_SMALL)), full((1, N_SMALL)),
                  full((1, GROUP)), full((1, GROUP))],
        out_specs=pl.BlockSpec((1, L, GROUP), lambda bi, c: (bi, c, 0)),
        out_shape=jax.ShapeDtypeStruct((b, s, GROUP), BF16),
        scratch_shapes=[pltpu.VMEM((8, cdim), F32), pltpu.VMEM((4 * SSM_STATE, GROUP), F32)],
        compiler_params=_params("arbitrary", "arbitrary"),
        name="ssd_mixer",
    )(u, small, conv_w, conv_b, dt_bias, a_neg, d_e, norm_g)


def _rwkv_kernel(p_ref, mu_ref, w0_ref, w2_ref, a0_ref, a2_ref, g2_ref, kk_ref, ka_ref, rk_ref,
                 lnw_ref, lnb_ref, o_ref, last_ref, h_ref):
    C = RWKV_CHUNK
    R = N_HEADS * C
    T = p_ref.shape[1]
    n_chunks = T // C
    c = pl.program_id(1)

    @pl.when(c == 0)
    def _():
        last_ref[...] = jnp.zeros(last_ref.shape, F32)
        h_ref[...] = jnp.zeros(h_ref.shape, F32)

    p = p_ref[0].astype(F32)
    row = _iota((T, 1), 0)
    p_prev = jnp.where(row >= 1, pltpu.roll(p, 1, 0), jnp.broadcast_to(last_ref[0:1, :], p.shape))
    last_ref[0:1, :] = p[T - 1:T, :]
    pm = p + (p_prev - p) * mu_ref[...]
    r = pm[:, 0:GROUP]
    k = pm[:, GROUP:2 * GROUP]
    v = pm[:, 2 * GROUP:3 * GROUP]
    xlow = pm[:, 3 * GROUP:]
    w_raw = -_softplus(-(w0_ref[...] + _dot(jnp.tanh(xlow).astype(BF16), w2_ref[...]))) - 0.5
    logw = -jnp.exp(w_raw)
    a = _sigmoid(a0_ref[...] + _dot(xlow.astype(BF16), a2_ref[...]))
    gate = _dot(_sigmoid(xlow).astype(BF16), g2_ref[...])
    kk = k * kk_ref[...]
    kk = kk * lax.rsqrt(_head_mean(kk * kk) * HEAD_DIM + 1e-12)
    k = k * (1.0 + (a - 1.0) * ka_ref[...])
    bonus = (_head_mean(r * k * rk_ref[...]) * HEAD_DIM) * v

    ti = _iota((T, T), 0)
    tj = _iota((T, T), 1)
    tri = ((ti >= tj) & (_div(ti, C) == _div(tj, C))).astype(BF16)
    lc = _dot3_left(tri, logw)
    e_pos = jnp.exp(lc)
    e_neg = jnp.exp(-lc)
    a_t = -kk * jnp.exp(lc - logw)
    b_t = (a * kk) * e_neg
    k_t = k * e_neg
    r_t = r * e_pos

    hl = _div(_iota((1, GROUP), 1), HEAD_DIM)
    ri = _iota((R, R), 0)
    ci = _iota((R, R), 1)
    same = _div(ri, C) == _div(ci, C)
    strict = same & (_mod(ri, C) > _mod(ci, C))
    incl = same & (_mod(ri, C) >= _mod(ci, C))
    eye = ri == ci
    chunks = range(n_chunks)

    def stack(x, n):
        xc = x[n * C:(n + 1) * C]
        return jnp.concatenate([jnp.where(hl == h, xc, 0.0) for h in range(N_HEADS)], axis=0)

    a_s = [stack(a_t, n) for n in chunks]
    b_s = [stack(b_t, n) for n in chunks]
    k_s = [stack(k_t, n) for n in chunks]
    r_s = [stack(r_t, n) for n in chunks]
    v_sb = [stack(v, n).astype(BF16) for n in chunks]
    pc = [e_pos[n * C + C - 1:n * C + C, :] for n in chunks]
    m_all = [_dot_nt(jnp.concatenate([a_s[n], r_s[n]], axis=0).astype(BF16),
                     jnp.concatenate([b_s[n], k_s[n]], axis=0).astype(BF16)) for n in chunks]
    l_p = [jnp.where(strict, m_all[n][0:R, 0:R], 0.0) for n in chunks]
    m_ak = [jnp.where(strict, m_all[n][0:R, R:], 0.0).astype(BF16) for n in chunks]
    m_rb = [jnp.where(incl, m_all[n][R:, 0:R], 0.0).astype(BF16) for n in chunks]
    m_rk = [jnp.where(incl, m_all[n][R:, R:], 0.0).astype(BF16) for n in chunks]
    t_m = [jnp.where(eye, 1.0, 0.0) + l_p[n] for n in chunks]
    for _ in range(int(math.log2(C)) - 1):
        l_pb = [l_p[n].astype(BF16) for n in chunks]
        l_p = [_dot(l_pb[n], l_pb[n]) for n in chunks]
        t_m = [t_m[n] + _dot(t_m[n].astype(BF16), l_p[n].astype(BF16)) for n in chunks]
    t_b = [t_m[n].astype(BF16) for n in chunks]
    mv = [_dot(m_ak[n], v_sb[n]).astype(BF16) for n in chunks]
    w_sb = [_dot(t_b[n], a_s[n].astype(BF16)).astype(BF16) for n in chunks]
    u0_b = [_dot(t_b[n], mv[n]).astype(BF16) for n in chunks]
    bh = [(b_s[n] * pc[n]).astype(BF16) for n in chunks]
    kh = [(k_s[n] * pc[n]).astype(BF16) for n in chunks]
    g_m = [(jnp.where(eye, jnp.broadcast_to(pc[n], (R, R)), 0.0) + _dot_tn(bh[n], w_sb[n])).astype(BF16)
           for n in chunks]
    d_m = [_dot_tn(bh[n], u0_b[n]) + _dot_tn(kh[n], v_sb[n]) for n in chunks]
    q_m = [(r_s[n] + _dot(m_rb[n], w_sb[n])).astype(BF16) for n in chunks]
    y0 = [_dot(m_rb[n], u0_b[n]) + _dot(m_rk[n], v_sb[n]) for n in chunks]
    state = h_ref[...]
    ys = []
    for n in chunks:
        h_b = state.astype(BF16)
        y_s = _dot(q_m[n], h_b) + y0[n]
        state = _dot(g_m[n], h_b) + d_m[n]
        ys.append(y_s[0:C] + y_s[C:2 * C] + y_s[2 * C:3 * C] + y_s[3 * C:4 * C])
    h_ref[...] = state
    y = jnp.concatenate(ys, axis=0)

    mean = _head_mean(y)
    yc = y - mean
    var = _head_mean(yc * yc)
    o = yc * lax.rsqrt(var + RWKV_LN_EPS) * lnw_ref[...] + lnb_ref[...] + bonus
    o_ref[0] = (o * gate).astype(o_ref.dtype)


def _rwkv(p, mu, w0, w2p, a0, a2p, g2p, k_k, k_a, r_k, ln_w, ln_b):
    b, s, _ = p.shape
    C = RWKV_CHUNK * RWKV_CHUNKS_PER_STEP
    full = lambda shape: pl.BlockSpec(shape, lambda bi, c: (0,) * len(shape))
    vec = full((1, GROUP))
    mat = full((128, GROUP))
    return pl.pallas_call(
        _rwkv_kernel,
        grid=(b, s // C),
        in_specs=[pl.BlockSpec((1, C, RWKV_PROJ), lambda bi, c: (bi, c, 0)), full((1, RWKV_PROJ)),
                  vec, mat, vec, mat, mat, vec, vec, vec, vec, vec],
        out_specs=pl.BlockSpec((1, C, GROUP), lambda bi, c: (bi, c, 0)),
        out_shape=jax.ShapeDtypeStruct((b, s, GROUP), BF16),
        scratch_shapes=[pltpu.VMEM((8, RWKV_PROJ), F32), pltpu.VMEM((GROUP, GROUP), F32)],
        compiler_params=_params("arbitrary", "arbitrary"),
        name="rwkv7_mix",
    )(p, mu, w0, w2p, a0, a2p, g2p, k_k, k_a, r_k, ln_w, ln_b)


def _out_proj_kernel(x_ref, y0_ref, y1_ref, y2_ref, y3_ref, w_ref, g_ref, o_ref):
    acc = _dot(y0_ref[...], w_ref[0:GROUP, :])
    acc = acc + _dot(y1_ref[...], w_ref[GROUP:2 * GROUP, :])
    acc = acc + _dot(y2_ref[...], w_ref[2 * GROUP:3 * GROUP, :])
    acc = acc + _dot(y3_ref[...], w_ref[3 * GROUP:4 * GROUP, :])
    ms = jnp.mean(acc * acc, axis=-1, keepdims=True)
    o_ref[...] = x_ref[...] + acc * lax.rsqrt(ms + NORM_EPS) * g_ref[...]


def _out_proj(x2, ys, w, g, tm=512):
    t = x2.shape[0]
    rows = lambda n: pl.BlockSpec((tm, n), lambda i: (i, 0))
    full = lambda shape: pl.BlockSpec(shape, lambda i: (0, 0))
    return pl.pallas_call(
        _out_proj_kernel,
        grid=(t // tm,),
        in_specs=[rows(D_MODEL)] + [rows(GROUP)] * 4 + [full((D_MODEL, D_MODEL)), full((1, D_MODEL))],
        out_specs=rows(D_MODEL),
        out_shape=jax.ShapeDtypeStruct((t, D_MODEL), F32),
        compiler_params=_params("parallel"),
        name="out_proj",
    )(x2, *ys, w, g)


def _ffn_kernel(x_ref, g1_ref, wg_ref, wu_ref, cw_ref, cb_ref, wd_ref, g2_ref, o_ref, tail_ref):
    tm = x_ref.shape[1]
    j = pl.program_id(1)

    @pl.when(j == 0)
    def _():
        tail_ref[...] = jnp.zeros(tail_ref.shape, F32)

    x = x_ref[0]
    ms = jnp.mean(x * x, axis=-1, keepdims=True)
    h = (x * lax.rsqrt(ms + NORM_EPS) * g1_ref[...]).astype(BF16)
    row8 = _iota((8, 1), 0)
    acc = jnp.zeros((tm, D_MODEL), F32)
    for c in range(D_FF // FFN_CHUNK):
        cs = slice(c * FFN_CHUNK, (c + 1) * FFN_CHUNK)
        gate = _dot(h, wg_ref[:, cs])
        up = _dot(h, wu_ref[:, cs])
        tail = tail_ref[:, cs]
        conv = gate * cw_ref[FFN_CONV - 1:FFN_CONV, cs] + cb_ref[:, cs]
        for d in range(1, FFN_CONV):
            rolled = pltpu.roll(gate, d, 0)
            head = jnp.where(row8 < d, pltpu.roll(tail, d, 0), rolled[0:8])
            shifted = jnp.concatenate([head, rolled[8:]], axis=0)
            conv = conv + shifted * cw_ref[FFN_CONV - 1 - d:FFN_CONV - d, cs]
        tail_ref[:, cs] = gate[tm - 8:tm, :]
        inner = 0.7978845608028654 * (conv + 0.044715 * conv * conv * conv)
        act = 0.5 * conv * (1.0 + jnp.tanh(inner))
        acc = acc + _dot((act * up).astype(BF16), wd_ref[cs, :])
    ms2 = jnp.mean(acc * acc, axis=-1, keepdims=True)
    o_ref[0] = x + acc * lax.rsqrt(ms2 + NORM_EPS) * g2_ref[...]


def _ffn(x, g1, wg, wu, cw, cb, wd, g2, tm=512):
    b, s, _ = x.shape
    tm = min(tm, s)
    full = lambda shape: pl.BlockSpec(shape, lambda bi, j: (0, 0), pipeline_mode=pl.Buffered(1))
    blk = pl.BlockSpec((1, tm, D_MODEL), lambda bi, j: (bi, j, 0))
    return pl.pallas_call(
        _ffn_kernel,
        grid=(b, s // tm),
        in_specs=[blk, full((1, D_MODEL)), full((D_MODEL, D_FF)), full((D_MODEL, D_FF)),
                  full((FFN_CONV, D_FF)), full((1, D_FF)), full((D_FF, D_MODEL)), full((1, D_MODEL))],
        out_specs=blk,
        out_shape=jax.ShapeDtypeStruct((b, s, D_MODEL), F32),
        scratch_shapes=[pltpu.VMEM((8, D_FF), F32)],
        compiler_params=_params("arbitrary", "arbitrary"),
        name="conv_glu_ffn",
    )(x, g1, wg, wu, cw, cb, wd, g2)


def _row(v):
    return v.reshape(1, -1).astype(F32)


def _pad_rows(m, start, total=128):
    return jnp.zeros((total, m.shape[1]), m.dtype).at[start:start + m.shape[0]].set(m)


def _head_groups(w):
    d = w.shape[0]
    w4 = w.reshape(d, N_HEADS, HEAD_DIM)
    return jnp.concatenate([w4, jnp.zeros_like(w4)], axis=2).reshape(d, AUG)


def _value_groups(w):
    d = w.shape[0]
    w4 = w.reshape(d, N_HEADS, HEAD_DIM)
    pad = jnp.zeros((d, N_HEADS, V_ROWS - HEAD_DIM), w.dtype)
    return jnp.concatenate([w4, pad], axis=2).reshape(d, N_HEADS * V_ROWS)


def _bias_lane_constants():
    lanes = jnp.arange(AUG)
    head, local = lanes // 128, lanes % 128
    slope = jnp.exp2(-(ALIBI_MAX_EXP / N_HEADS) * (head + 1).astype(F32))
    at = lambda k: (local == AUG_LANE + k).astype(F32)
    zero = jnp.zeros((AUG,), F32)
    fox_q = -(at(0) + at(1) + at(2))
    c_parts = [p.astype(F32) for p in _split3(slope * LOG2E)]
    c_at = lambda base: sum(c_parts[k] * at(base + k) for k in range(3))
    ones_at = lambda base: at(base) + at(base + 1) + at(base + 2)
    rows = jnp.stack([jnp.full((AUG,), DIFF_QK ** -0.5 * LOG2E, F32), -ones_at(0), -ones_at(3),
                      c_at(6) + c_at(9), ones_at(6), ones_at(9), c_at(0) + c_at(3),
                      jnp.full((AUG,), HEAD_DIM ** -0.5 * LOG2E, F32), fox_q] + [zero] * 7)
    src = jnp.arange(N_SMALL)
    place = jnp.stack([((src[:, None] == 4 + head[None, :]) & (local[None, :] == AUG_LANE + k)).astype(BF16)
                       for k in range(3)])
    return rows, place


def kernel(x, norm_mix_pre, norm_mix_post, norm_ffn_pre, norm_ffn_post, w_in, w_out, diff_lambda_q1, diff_lambda_k1, diff_lambda_q2, diff_lambda_k2, diff_subln, ssm_conv_w, ssm_conv_b, ssm_dt_bias, ssm_a_log, ssm_d, ssm_norm, fox_f_bias, fox_norm, rwkv_mu, rwkv_w0, rwkv_w2, rwkv_a0, rwkv_a2, rwkv_g2, rwkv_k_k, rwkv_k_a, rwkv_r_k, rwkv_ln_w, rwkv_ln_b, ffn_w_gate, ffn_w_up, ffn_conv_w, ffn_conv_b, ffn_w_down):
    b, s, d = x.shape
    t = b * s
    depth = w_in.shape[0]
    tq = min(ATT_BLOCK, s)
    nq = s // tq
    o_dt = 768 + 256 + 768
    o_fq = o_dt + 4
    o_ff = o_fq + 768
    o_rp = o_ff + 4
    aug_rows, place = _bias_lane_constants()
    for l in range(depth):
        wl = w_in[l]
        w_main = jnp.concatenate(
            [_head_groups(wl[:, 256:512]), wl[:, 768:o_dt],
             _head_groups(wl[:, o_fq + 256:o_fq + 512]), wl[:, o_rp:]], axis=1).astype(BF16)
        w_t = jnp.concatenate(
            [_head_groups(wl[:, 0:256]), _value_groups(wl[:, 512:768]),
             _head_groups(wl[:, o_fq:o_fq + 256]), _value_groups(wl[:, o_fq + 512:o_ff])], axis=1).T.astype(BF16)
        w_small = jnp.concatenate([wl[:, o_dt:o_fq], wl[:, o_ff:o_rp],
                                   jnp.zeros((d, N_SMALL - 8), F32)], axis=1).astype(BF16)
        fbias = jnp.zeros((1, N_SMALL), F32).at[0, 4:8].set(fox_f_bias[l])
        (kd, u_ssm, kf, u_rwkv, u_small, qd_t, vd_t, qf_t, vf_t) = _in_proj(
            x.reshape(t, d), _row(norm_mix_pre[l]), w_main, w_t, w_small, aug_rows, fbias, place, b, s, tq)
        u_ssm = u_ssm.reshape(b, s, -1)
        u_rwkv = u_rwkv.reshape(b, s, -1)
        u_small = u_small.reshape(b, s, -1)

        lam_init = 0.8 - 0.6 * math.exp(-0.3 * l)
        g_diff = jnp.zeros((8, GROUP), F32).at[0].set(jnp.tile(diff_subln[l], N_HEADS)).at[1].set(lam_init)
        y_diff = _attention("diff", qd_t, kd.reshape(b, s, AUG), vd_t,
                            [_row(diff_lambda_q1[l]), _row(diff_lambda_k1[l]),
                             _row(diff_lambda_q2[l]), _row(diff_lambda_k2[l])], g_diff, tq)
        g_fox = jnp.zeros((8, GROUP), F32).at[0].set(jnp.tile(fox_norm[l], N_HEADS))
        y_fox = _attention("fox", qf_t, kf.reshape(b, s, AUG), vf_t, [], g_fox, tq)

        pad4 = lambda v: jnp.zeros((1, N_SMALL), F32).at[0, 0:4].set(v)
        y_ssm = _ssd(u_ssm, u_small, ssm_conv_w[l], _row(ssm_conv_b[l]), pad4(ssm_dt_bias[l]),
                     pad4(-jnp.exp(ssm_a_log[l])), _row(jnp.repeat(ssm_d[l], HEAD_DIM)), _row(ssm_norm[l]))

        y_rwkv = _rwkv(u_rwkv, _row(rwkv_mu[l]), _row(rwkv_w0[l]),
                       _pad_rows(rwkv_w2[l], 0).astype(BF16), _row(rwkv_a0[l]),
                       _pad_rows(rwkv_a2[l], 32).astype(BF16), _pad_rows(rwkv_g2[l], 64).astype(BF16),
                       _row(rwkv_k_k[l]), _row(rwkv_k_a[l]), _row(rwkv_r_k[l]),
                       _row(rwkv_ln_w[l]), _row(rwkv_ln_b[l]))

        flat = lambda y: y.reshape(t, GROUP)
        x = _out_proj(x.reshape(t, d), [flat(y_diff), flat(y_ssm), flat(y_fox), flat(y_rwkv)],
                      w_out[l].astype(BF16), _row(norm_mix_post[l])).reshape(b, s, d)

        x = _ffn(x, _row(norm_ffn_pre[l]), ffn_w_gate[l].astype(BF16), ffn_w_up[l].astype(BF16),
                 ffn_conv_w[l], _row(ffn_conv_b[l]), ffn_w_down[l].astype(BF16), _row(norm_ffn_post[l]))
    return x
```

```python
import functools
import math

import jax
import jax.numpy as jnp
from jax import lax
from jax.experimental import pallas as pl
from jax.experimental.pallas import tpu as pltpu

F32 = jnp.float32
BF16 = jnp.bfloat16

D_MODEL = 1024
GROUP = 256
N_HEADS = 4
HEAD_DIM = 64
DIFF_QK = 32
NORM_EPS = 1e-6
ALIBI_MAX_EXP = 8.0
SSM_STATE = 128
SSM_CONV = 4
SSM_CHUNK = 128
RWKV_CHUNK = 64
RWKV_CHUNKS_PER_STEP = 4
RWKV_PROJ = 896
RWKV_LN_EPS = 64e-5
D_FF = 2816
FFN_CONV = 3
FFN_CHUNK = 1408
AUG = 2 * GROUP
AUG_LANE = HEAD_DIM
V_ROWS = HEAD_DIM + 16
LOG2E = 1.4426950408889634
TCOLS = (0, AUG, AUG + N_HEADS * V_ROWS, 2 * AUG + N_HEADS * V_ROWS, 2 * AUG + 2 * N_HEADS * V_ROWS)
N_T = TCOLS[-1]
COLS = (0, AUG, AUG + 4 * GROUP, 2 * AUG + 4 * GROUP, 2 * AUG + 4 * GROUP + RWKV_PROJ)
N_MAIN = COLS[-1]
N_SMALL = 128
ATT_BLOCK = 512
ATT_UNIT_WIDTH = 512
VMEM_LIMIT = 56 * 1024 * 1024


def _dot(a, b):
    return jnp.dot(a, b, preferred_element_type=F32)


def _dot_nt(a, b):
    return lax.dot_general(a, b, (((1,), (1,)), ((), ())), preferred_element_type=F32)


def _dot_tn(a, b):
    return lax.dot_general(a, b, (((0,), (0,)), ((), ())), preferred_element_type=F32)


def _split3(x):
    h1 = x.astype(BF16)
    r1 = x - h1.astype(F32)
    h2 = r1.astype(BF16)
    h3 = (r1 - h2.astype(F32)).astype(BF16)
    return h1, h2, h3


def _dot3_left(m_bf16, x):
    h1, h2, h3 = _split3(x)
    return _dot(m_bf16, h1) + _dot(m_bf16, h2) + _dot(m_bf16, h3)


def _dot3_right(x, m_bf16):
    h1, h2, h3 = _split3(x)
    return _dot(h1, m_bf16) + _dot(h2, m_bf16) + _dot(h3, m_bf16)


def _iota(shape, dim):
    return lax.broadcasted_iota(jnp.int32, shape, dim)


def _div(x, n):
    return lax.shift_right_logical(x, int(math.log2(n)))


def _mod(x, n):
    return lax.bitwise_and(x, n - 1)


def _softplus(x):
    return jnp.maximum(x, 0.0) + jnp.log(1.0 + jnp.exp(-jnp.abs(x)))


def _sigmoid(x):
    return 1.0 / (1.0 + jnp.exp(-x))


def _silu(x):
    return x * _sigmoid(x)


def _params(*sem):
    return pltpu.CompilerParams(dimension_semantics=sem, vmem_limit_bytes=VMEM_LIMIT)


def _in_proj_kernel(x_ref, g_ref, w_ref, wt_ref, ws_ref, aug_ref, augt_ref, vones_ref, fb_ref, place_ref,
                    kd_ref, os_ref, kf_ref, or_ref, osm_ref, qdt_ref, vdt_ref, qft_ref, vft_ref,
                    carry_ref, *, tm, tq, blocks_per_seq):
    i = pl.program_id(0)

    @pl.when(i % blocks_per_seq == 0)
    def _():
        carry_ref[...] = jnp.zeros(carry_ref.shape, F32)

    x = x_ref[...]
    ms = jnp.mean(x * x, axis=-1, keepdims=True)
    h = (x * lax.rsqrt(ms + NORM_EPS) * g_ref[...]).astype(BF16)
    proj = lambda n: _dot(h, w_ref[:, COLS[n]:COLS[n + 1]])
    proj_t = lambda n: _dot_nt(wt_ref[TCOLS[n]:TCOLS[n + 1], :], h)
    col = lambda k: augt_ref[:, k:k + 1]
    pos = _mod(_iota((tm, 1), 0), tq)
    lo = _mod(pos, 256).astype(F32)
    hi = pos.astype(F32) - lo
    pos_t = _mod(_iota((1, tm), 1), tq)
    lo_t = _mod(pos_t, 256).astype(F32)
    hi_t = pos_t.astype(F32) - lo_t
    qdt_ref[0] = (proj_t(0) * col(0) + col(1) * lo_t + col(2) * hi_t + col(3)).astype(BF16)
    vdt_ref[0, 0] = (proj_t(1) + vones_ref[...]).astype(BF16)
    qft_ref[0] = (proj_t(2) * col(7) + col(8)).astype(BF16)
    vft_ref[0, 0] = (proj_t(3) + vones_ref[...]).astype(BF16)
    kd_ref[...] = (proj(0) + lo * aug_ref[4:5, :] + hi * aug_ref[5:6, :] + aug_ref[6:7, :]).astype(BF16)
    os_ref[...] = proj(1).astype(BF16)
    small = _dot(h, ws_ref[...])
    osm_ref[...] = small
    f = small + fb_ref[...]
    ls = jnp.minimum(f, 0.0) - jnp.log(1.0 + jnp.exp(-jnp.abs(f)))
    tri = (_iota((tm, tm), 0) >= _iota((tm, tm), 1)).astype(BF16)
    cum = _dot3_left(tri, ls) + carry_ref[0:1, :]
    carry_ref[0:1, :] = cum[tm - 1:tm, :]
    c1, c2, c3 = _split3(cum * LOG2E)
    kbias = _dot(c1, place_ref[0]) + _dot(c2, place_ref[1]) + _dot(c3, place_ref[2])
    kf_ref[...] = (proj(2) + kbias).astype(BF16)
    or_ref[...] = proj(3).astype(BF16)


def _in_proj(x2, g, w_main, w_t, w_small, aug_rows, fbias, place, batch, seq, tq):
    t = x2.shape[0]
    tm = tq
    bps = seq // tm
    nv = N_HEADS * V_ROWS
    full = lambda shape: pl.BlockSpec(shape, lambda i: (0,) * len(shape))
    rows = lambda n: pl.BlockSpec((tm, n), lambda i: (i, 0))
    q_t = pl.BlockSpec((1, AUG, tm), lambda i: (i // bps, 0, i % bps))
    v_t = pl.BlockSpec((1, 1, nv, tm), lambda i: (i // bps, i % bps, 0, 0))
    widths = (AUG, 4 * GROUP, AUG, RWKV_PROJ)
    lanes = jnp.arange(nv) % V_ROWS
    vones = (lanes >= HEAD_DIM).astype(F32).reshape(nv, 1)
    q_shape = jax.ShapeDtypeStruct((batch, AUG, seq), BF16)
    v_shape = jax.ShapeDtypeStruct((batch, bps, nv, tm), BF16)
    return pl.pallas_call(
        functools.partial(_in_proj_kernel, tm=tm, tq=tq, blocks_per_seq=bps),
        grid=(t // tm,),
        in_specs=[rows(D_MODEL), full((1, D_MODEL)), full((D_MODEL, N_MAIN)), full((N_T, D_MODEL)),
                  full((D_MODEL, N_SMALL)), full((16, AUG)), full((AUG, 16)), full((nv, 1)),
                  full((1, N_SMALL)), full((3, N_SMALL, AUG))],
        out_specs=[rows(n) for n in widths] + [rows(N_SMALL), q_t, v_t, q_t, v_t],
        out_shape=[jax.ShapeDtypeStruct((t, n), BF16) for n in widths]
        + [jax.ShapeDtypeStruct((t, N_SMALL), F32), q_shape, v_shape, q_shape, v_shape],
        scratch_shapes=[pltpu.VMEM((8, N_SMALL), F32)],
        compiler_params=_params("arbitrary"),
        name="in_proj",
    )(x2, g, w_main, w_t, w_small, aug_rows, aug_rows.T, vones, fbias, place)


def _head_mean_matrix():
    same = _div(_iota((GROUP, GROUP), 0), HEAD_DIM) == _div(_iota((GROUP, GROUP), 1), HEAD_DIM)
    return jnp.where(same, 1.0 / HEAD_DIM, 0.0).astype(BF16)


def _head_mean(x):
    return _dot(x.astype(BF16), _head_mean_matrix())


def _attn_kernel(*refs, kind, tq):
    if kind == "diff":
        (qt_ref, k_ref, vt_ref, lq1_ref, lk1_ref, lq2_ref, lk2_ref, g_ref, o_ref,
         qs_ref, m_ref, acc_ref) = refs
        n_c = 2
    else:
        qt_ref, k_ref, vt_ref, g_ref, o_ref, qs_ref, m_ref, acc_ref = refs
        n_c = 1
    tk = tq
    i = pl.program_id(1)

    sub = _iota((128, 1), 0)
    for h in range(N_HEADS):
        qh = qt_ref[0, h * 128:(h + 1) * 128, :]
        if kind == "diff":
            zero = jnp.zeros_like(qh)
            qs_ref[h, :, 0:tq] = jnp.where((sub >= DIFF_QK) & (sub < 2 * DIFF_QK), zero, qh)
            qs_ref[h, :, tq:2 * tq] = jnp.where(sub < DIFF_QK, zero, qh)
        else:
            qs_ref[h] = qh
    m_ref[...] = jnp.full(m_ref.shape, -jnp.inf, F32)
    acc_ref[...] = jnp.zeros(acc_ref.shape, F32)
    wu = min(ATT_UNIT_WIDTH, tq)
    units = [(h, l0) for h in range(N_HEADS) for l0 in range(0, n_c * tq, wu)]
    kpos = _iota((tk, wu), 0)
    qcol = _iota((tk, wu), 1)

    def run(kv_steps):
        work = [(j, masked, h, l0) for j, masked in kv_steps for h, l0 in units]

        def logits(unit):
            j, _, h, l0 = unit
            start = pl.multiple_of(j * tk, tk)
            return _dot(k_ref[0, pl.ds(start, tk), h * 128:(h + 1) * 128], qs_ref[h, :, l0:l0 + wu])

        def stats(unit, s):
            j, masked, h, l0 = unit
            slope2 = LOG2E * 2.0 ** (-(ALIBI_MAX_EXP / N_HEADS) * (h + 1))
            off = -slope2 * ((i - j) * tq).astype(F32) if kind == "diff" else 0.0
            if masked:
                s = jnp.where(kpos <= qcol + (l0 % tq), s, -jnp.inf)
            m_old = m_ref[h, :, l0:l0 + wu]
            m_new = jnp.maximum(m_old, jnp.max(s, axis=0, keepdims=True) + off)
            m_ref[h, :, l0:l0 + wu] = m_new
            return s, jnp.exp2(m_old - m_new), m_new - off

        def accumulate(unit, s, alpha, shift):
            j, _, h, l0 = unit
            p = jnp.exp2(s - shift).astype(BF16)
            v_h = vt_ref[0, j, h * V_ROWS:(h + 1) * V_ROWS, :]
            acc_ref[h, :, l0:l0 + wu] = acc_ref[h, :, l0:l0 + wu] * alpha + _dot(v_h, p)

        n_work = len(work)
        s_of = {0: logits(work[0])}
        if n_work > 1:
            s_of[1] = logits(work[1])
        st_of = {0: stats(work[0], s_of.pop(0))}
        for n in range(n_work):
            if n + 2 < n_work:
                s_of[n + 2] = logits(work[n + 2])
            if n + 1 < n_work:
                st_of[n + 1] = stats(work[n + 1], s_of.pop(n + 1))
            accumulate(work[n], *st_of.pop(n))

    def body(t, carry):
        run([(2 * t, False), (2 * t + 1, False)])
        return carry

    lax.fori_loop(0, i // 2, body, 0)

    @pl.when(i % 2 == 1)
    def _():
        run([(i - 1, False), (i, True)])

    @pl.when(i % 2 == 0)
    def _():
        run([(i, True)])

    parts = []
    if kind == "diff":
        lam_init = g_ref[1:2, 0:1]
        lam = (jnp.exp(jnp.sum(lq1_ref[...] * lk1_ref[...], axis=1, keepdims=True))
               - jnp.exp(jnp.sum(lq2_ref[...] * lk2_ref[...], axis=1, keepdims=True)) + lam_init)
        for h in range(N_HEADS):
            on = acc_ref[h, 0:HEAD_DIM, :] * (1.0 / acc_ref[h, HEAD_DIM:HEAD_DIM + 1, :])
            parts.append(on[:, 0:tq] - lam * on[:, tq:2 * tq])
    else:
        for h in range(N_HEADS):
            parts.append(acc_ref[h, 0:HEAD_DIM, :] * (1.0 / acc_ref[h, HEAD_DIM:HEAD_DIM + 1, :]))
    o = jnp.concatenate(parts, axis=0).T
    ms = _head_mean(o * o)
    y = o * lax.rsqrt(ms + NORM_EPS) * g_ref[0:1, :]
    if kind == "diff":
        y = y * (1.0 - lam_init)
    o_ref[0] = y.astype(o_ref.dtype)


def _attention(kind, q_t, k_aug, v_t, extra, gains, tq):
    b, s, _ = k_aug.shape
    nq = s // tq
    w = (2 if kind == "diff" else 1) * tq
    extra_specs = [pl.BlockSpec((1, DIFF_QK), lambda bi, i: (0, 0))] * len(extra)
    once = pl.Buffered(1)
    return pl.pallas_call(
        functools.partial(_attn_kernel, kind=kind, tq=tq),
        grid=(b, nq),
        in_specs=[pl.BlockSpec((1, AUG, tq), lambda bi, i: (bi, 0, i)),
                  pl.BlockSpec((1, s, AUG), lambda bi, i: (bi, 0, 0), pipeline_mode=once),
                  pl.BlockSpec((1, nq, N_HEADS * V_ROWS, tq), lambda bi, i: (bi, 0, 0, 0), pipeline_mode=once)]
        + extra_specs + [pl.BlockSpec((8, GROUP), lambda bi, i: (0, 0))],
        out_specs=pl.BlockSpec((1, tq, GROUP), lambda bi, i: (bi, i, 0)),
        out_shape=jax.ShapeDtypeStruct((b, s, GROUP), BF16),
        scratch_shapes=[pltpu.VMEM((N_HEADS, 128, w), BF16), pltpu.VMEM((N_HEADS, 1, w), F32),
                        pltpu.VMEM((N_HEADS, V_ROWS, w), F32)],
        compiler_params=_params("arbitrary", "arbitrary"),
        name=kind + "_attention",
    )(q_t, k_aug, v_t, *extra, gains)


def _ssd_kernel(u_ref, sm_ref, cw_ref, cb_ref, dtb_ref, a_ref, d_ref, g_ref, o_ref,
                prev_ref, state_ref):
    L = SSM_CHUNK
    c = pl.program_id(1)

    @pl.when(c == 0)
    def _():
        prev_ref[...] = jnp.zeros(prev_ref.shape, F32)
        state_ref[...] = jnp.zeros(state_ref.shape, F32)

    u = u_ref[0].astype(F32)
    z = u[:, 0:GROUP]
    raw = u[:, GROUP:]
    tail = prev_ref[...]
    row8 = _iota((8, 1), 0)
    xbc = raw * cw_ref[SSM_CONV - 1:SSM_CONV, :] + cb_ref[...]
    for d in range(1, SSM_CONV):
        rolled = pltpu.roll(raw, d, 0)
        head = jnp.where(row8 < d, pltpu.roll(tail, d, 0), rolled[0:8])
        shifted = jnp.concatenate([head, rolled[8:]], axis=0)
        xbc = xbc + shifted * cw_ref[SSM_CONV - 1 - d:SSM_CONV - d, :]
    prev_ref[...] = raw[L - 8:L, :]
    xbc = _silu(xbc)
    xs = xbc[:, 0:GROUP]
    bm = xbc[:, GROUP:2 * GROUP]
    cm = xbc[:, 2 * GROUP:3 * GROUP]

    dt = _softplus(sm_ref[0] + dtb_ref[...])
    da = dt * a_ref[...]
    rep = (_iota((128, 4 * 128), 0) == _div(_iota((128, 4 * 128), 1), 128)).astype(BF16)
    exp_h = (_iota((128, GROUP), 0) == _div(_iota((128, GROUP), 1), HEAD_DIM)).astype(BF16)
    dt_e = _dot3_right(dt, exp_h)
    tri = (_iota((L, L), 0) >= _iota((L, L), 1)).astype(BF16)
    acs = _dot3_right(_dot3_left(tri, da), rep)
    last = acs[L - 1:L, :]
    causal = _iota((L, L), 0) >= _iota((L, L), 1)

    xdt = (xs * dt_e).astype(BF16)
    scores, bd, cd = [], [], []
    for h in range(N_HEADS):
        g = h // 2
        col = acs[:, h * 128:(h + 1) * 128]
        b_g = bm[:, g * 128:(g + 1) * 128]
        c_g = cm[:, g * 128:(g + 1) * 128]
        if h % 2 == 0:
            cb = _dot_nt(c_g.astype(BF16), b_g.astype(BF16))
        decay = jnp.exp(jnp.where(causal, col - col.T, -jnp.inf))
        scores.append((cb * decay).astype(BF16))
        bd.append((b_g * jnp.exp(last[:, h * 128:(h + 1) * 128] - col)).astype(BF16))
        cd.append((c_g * jnp.exp(col)).astype(BF16))
    y_st = _dot(jnp.concatenate(scores, axis=0), xdt)
    hl = _div(_iota((1, GROUP), 1), HEAD_DIM)
    y = jnp.zeros((L, GROUP), F32)
    for h in range(N_HEADS):
        y = y + jnp.where(hl == h, y_st[h * L:(h + 1) * L, :], 0.0)
    state = state_ref[...]
    y = y + _dot(jnp.concatenate(cd, axis=1), state.astype(BF16))
    st_new = _dot_tn(jnp.concatenate(bd, axis=1), xdt)
    blk = _div(_iota((4 * 128, GROUP), 0), 128) == _div(_iota((4 * 128, GROUP), 1), HEAD_DIM)
    chunk_decay = jnp.concatenate(
        [jnp.broadcast_to(jnp.exp(last[:, h * 128:h * 128 + 1]), (128, GROUP)) for h in range(N_HEADS)], axis=0)
    state_ref[...] = state * chunk_decay + jnp.where(blk, st_new, 0.0)

    y = y + xs * d_ref[...]
    yg = y * _silu(z)
    sq = yg * yg
    half = GROUP // 2
    ms0 = jnp.mean(sq[:, 0:half], axis=1, keepdims=True)
    ms1 = jnp.mean(sq[:, half:], axis=1, keepdims=True)
    ms = jnp.where(_iota((1, GROUP), 1) < half, ms0, ms1)
    o_ref[0] = (yg * lax.rsqrt(ms + NORM_EPS) * g_ref[...]).astype(o_ref.dtype)


def _ssd(u, small, conv_w, conv_b, dt_bias, a_neg, d_e, norm_g):
    b, s, _ = u.shape
    L = SSM_CHUNK
    cdim = 3 * GROUP
    full = lambda shape: pl.BlockSpec(shape, lambda bi, c: (0,) * len(shape))
    return pl.pallas_call(
        _ssd_kernel,
        grid=(b, s // L),
        in_specs=[pl.BlockSpec((1, L, 4 * GROUP), lambda bi, c: (bi, c, 0)),
                  pl.BlockSpec((1, L, N_SMALL), lambda bi, c: (bi, c, 0)),
                  full((SSM_CONV, cdim)), full((1, cdim)), full((1, N_SMALL)), full((1, N_SMALL)),
                  full((1, GROUP)), full((1, GROUP))],
        out_specs=pl.BlockSpec((1, L, GROUP), lambda bi, c: (bi, c, 0)),
        out_shape=jax.ShapeDtypeStruct((b, s, GROUP), BF16),
        scratch_shapes=[pltpu.VMEM((8, cdim), F32), pltpu.VMEM((4 * SSM_STATE, GROUP), F32)],
        compiler_params=_params("arbitrary", "arbitrary"),
        name="ssd_mixer",
    )(u, small, conv_w, conv_b, dt_bias, a_neg, d_e, norm_g)


def _rwkv_kernel(p_ref, mu_ref, w0_ref, w2_ref, a0_ref, a2_ref, g2_ref, kk_ref, ka_ref, rk_ref,
                 lnw_ref, lnb_ref, o_ref, last_ref, h_ref):
    C = RWKV_CHUNK
    R = N_HEADS * C
    T = p_ref.shape[1]
    n_chunks = T // C
    c = pl.program_id(1)

    @pl.when(c == 0)
    def _():
        last_ref[...] = jnp.zeros(last_ref.shape, F32)
        h_ref[...] = jnp.zeros(h_ref.shape, F32)

    p = p_ref[0].astype(F32)
    row = _iota((T, 1), 0)
    p_prev = jnp.where(row >= 1, pltpu.roll(p, 1, 0), jnp.broadcast_to(last_ref[0:1, :], p.shape))
    last_ref[0:1, :] = p[T - 1:T, :]
    pm = p + (p_prev - p) * mu_ref[...]
    r = pm[:, 0:GROUP]
    k = pm[:, GROUP:2 * GROUP]
    v = pm[:, 2 * GROUP:3 * GROUP]
    xlow = pm[:, 3 * GROUP:]
    w_raw = -_softplus(-(w0_ref[...] + _dot(jnp.tanh(xlow).astype(BF16), w2_ref[...]))) - 0.5
    logw = -jnp.exp(w_raw)
    a = _sigmoid(a0_ref[...] + _dot(xlow.astype(BF16), a2_ref[...]))
    gate = _dot(_sigmoid(xlow).astype(BF16), g2_ref[...])
    kk = k * kk_ref[...]
    kk = kk * lax.rsqrt(_head_mean(kk * kk) * HEAD_DIM + 1e-12)
    k = k * (1.0 + (a - 1.0) * ka_ref[...])
    bonus = (_head_mean(r * k * rk_ref[...]) * HEAD_DIM) * v

    ti = _iota((T, T), 0)
    tj = _iota((T, T), 1)
    tri = ((ti >= tj) & (_div(ti, C) == _div(tj, C))).astype(BF16)
    lc = _dot3_left(tri, logw)
    e_pos = jnp.exp(lc)
    e_neg = jnp.exp(-lc)
    a_t = -kk * jnp.exp(lc - logw)
    b_t = (a * kk) * e_neg
    k_t = k * e_neg
    r_t = r * e_pos

    hl = _div(_iota((1, GROUP), 1), HEAD_DIM)
    ri = _iota((R, R), 0)
    ci = _iota((R, R), 1)
    same = _div(ri, C) == _div(ci, C)
    strict = same & (_mod(ri, C) > _mod(ci, C))
    incl = same & (_mod(ri, C) >= _mod(ci, C))
    eye = ri == ci
    chunks = range(n_chunks)

    def stack(x, n):
        xc = x[n * C:(n + 1) * C]
        return jnp.concatenate([jnp.where(hl == h, xc, 0.0) for h in range(N_HEADS)], axis=0)

    a_s = [stack(a_t, n) for n in chunks]
    b_s = [stack(b_t, n) for n in chunks]
    k_s = [stack(k_t, n) for n in chunks]
    r_s = [stack(r_t, n) for n in chunks]
    v_sb = [stack(v, n).astype(BF16) for n in chunks]
    pc = [e_pos[n * C + C - 1:n * C + C, :] for n in chunks]
    m_all = [_dot_nt(jnp.concatenate([a_s[n], r_s[n]], axis=0).astype(BF16),
                     jnp.concatenate([b_s[n], k_s[n]], axis=0).astype(BF16)) for n in chunks]
    l_p = [jnp.where(strict, m_all[n][0:R, 0:R], 0.0) for n in chunks]
    m_ak = [jnp.where(strict, m_all[n][0:R, R:], 0.0).astype(BF16) for n in chunks]
    m_rb = [jnp.where(incl, m_all[n][R:, 0:R], 0.0).astype(BF16) for n in chunks]
    m_rk = [jnp.where(incl, m_all[n][R:, R:], 0.0).astype(BF16) for n in chunks]
    t_m = [jnp.where(eye, 1.0, 0.0) + l_p[n] for n in chunks]
    for _ in range(int(math.log2(C)) - 1):
        l_pb = [l_p[n].astype(BF16) for n in chunks]
        l_p = [_dot(l_pb[n], l_pb[n]) for n in chunks]
        t_m = [t_m[n] + _dot(t_m[n].astype(BF16), l_p[n].astype(BF16)) for n in chunks]
    t_b = [t_m[n].astype(BF16) for n in chunks]
    mv = [_dot(m_ak[n], v_sb[n]).astype(BF16) for n in chunks]
    w_sb = [_dot(t_b[n], a_s[n].astype(BF16)).astype(BF16) for n in chunks]
    u0_b = [_dot(t_b[n], mv[n]).astype(BF16) for n in chunks]
    bh = [(b_s[n] * pc[n]).astype(BF16) for n in chunks]
    kh = [(k_s[n] * pc[n]).astype(BF16) for n in chunks]
    g_m = [(jnp.where(eye, jnp.broadcast_to(pc[n], (R, R)), 0.0) + _dot_tn(bh[n], w_sb[n])).astype(BF16)
           for n in chunks]
    d_m = [_dot_tn(bh[n], u0_b[n]) + _dot_tn(kh[n], v_sb[n]) for n in chunks]
    q_m = [(r_s[n] + _dot(m_rb[n], w_sb[n])).astype(BF16) for n in chunks]
    y0 = [_dot(m_rb[n], u0_b[n]) + _dot(m_rk[n], v_sb[n]) for n in chunks]
    state = h_ref[...]
    ys = []
    for n in chunks:
        h_b = state.astype(BF16)
        y_s = _dot(q_m[n], h_b) + y0[n]
        state = _dot(g_m[n], h_b) + d_m[n]
        ys.append(y_s[0:C] + y_s[C:2 * C] + y_s[2 * C:3 * C] + y_s[3 * C:4 * C])
    h_ref[...] = state
    y = jnp.concatenate(ys, axis=0)

    mean = _head_mean(y)
    yc = y - mean
    var = _head_mean(yc * yc)
    o = yc * lax.rsqrt(var + RWKV_LN_EPS) * lnw_ref[...] + lnb_ref[...] + bonus
    o_ref[0] = (o * gate).astype(o_ref.dtype)


def _rwkv(p, mu, w0, w2p, a0, a2p, g2p, k_k, k_a, r_k, ln_w, ln_b):
    b, s, _ = p.shape
    C = RWKV_CHUNK * RWKV_CHUNKS_PER_STEP
    full = lambda shape: pl.BlockSpec(shape, lambda bi, c: (0,) * len(shape))
    vec = full((1, GROUP))
    mat = full((128, GROUP))
    return pl.pallas_call(
        _rwkv_kernel,
        grid=(b, s // C),
        in_specs=[pl.BlockSpec((1, C, RWKV_PROJ), lambda bi, c: (bi, c, 0)), full((1, RWKV_PROJ)),
                  vec, mat, vec, mat, mat, vec, vec, vec, vec, vec],
        out_specs=pl.BlockSpec((1, C, GROUP), lambda bi, c: (bi, c, 0)),
        out_shape=jax.ShapeDtypeStruct((b, s, GROUP), BF16),
        scratch_shapes=[pltpu.VMEM((8, RWKV_PROJ), F32), pltpu.VMEM((GROUP, GROUP), F32)],
        compiler_params=_params("arbitrary", "arbitrary"),
        name="rwkv7_mix",
    )(p, mu, w0, w2p, a0, a2p, g2p, k_k, k_a, r_k, ln_w, ln_b)


def _ffn_kernel(x_ref, y0_ref, y1_ref, y2_ref, y3_ref, wo_ref, go_ref,
                g1_ref, wg_ref, wu_ref, cw_ref, cb_ref, wd_ref, g2_ref, o_ref, tail_ref):
    tm = x_ref.shape[1]
    j = pl.program_id(1)

    @pl.when(j == 0)
    def _():
        tail_ref[...] = jnp.zeros(tail_ref.shape, F32)

    mix = _dot(y0_ref[0], wo_ref[0:GROUP, :])
    mix = mix + _dot(y1_ref[0], wo_ref[GROUP:2 * GROUP, :])
    mix = mix + _dot(y2_ref[0], wo_ref[2 * GROUP:3 * GROUP, :])
    mix = mix + _dot(y3_ref[0], wo_ref[3 * GROUP:4 * GROUP, :])
    ms0 = jnp.mean(mix * mix, axis=-1, keepdims=True)
    x = x_ref[0] + mix * lax.rsqrt(ms0 + NORM_EPS) * go_ref[...]
    ms = jnp.mean(x * x, axis=-1, keepdims=True)
    h = (x * lax.rsqrt(ms + NORM_EPS) * g1_ref[...]).astype(BF16)
    row8 = _iota((8, 1), 0)
    acc = jnp.zeros((tm, D_MODEL), F32)
    for c in range(D_FF // FFN_CHUNK):
        cs = slice(c * FFN_CHUNK, (c + 1) * FFN_CHUNK)
        gate = _dot(h, wg_ref[:, cs])
        up = _dot(h, wu_ref[:, cs])
        tail = tail_ref[:, cs]
        conv = gate * cw_ref[FFN_CONV - 1:FFN_CONV, cs] + cb_ref[:, cs]
        for d in range(1, FFN_CONV):
            rolled = pltpu.roll(gate, d, 0)
            head = jnp.where(row8 < d, pltpu.roll(tail, d, 0), rolled[0:8])
            shifted = jnp.concatenate([head, rolled[8:]], axis=0)
            conv = conv + shifted * cw_ref[FFN_CONV - 1 - d:FFN_CONV - d, cs]
        tail_ref[:, cs] = gate[tm - 8:tm, :]
        inner = 0.7978845608028654 * (conv + 0.044715 * conv * conv * conv)
        act = 0.5 * conv * (1.0 + jnp.tanh(inner))
        acc = acc + _dot((act * up).astype(BF16), wd_ref[cs, :])
    ms2 = jnp.mean(acc * acc, axis=-1, keepdims=True)
    o_ref[0] = x + acc * lax.rsqrt(ms2 + NORM_EPS) * g2_ref[...]


def _out_proj_ffn(x, ys, wo, go, g1, wg, wu, cw, cb, wd, g2, tm=512):
    b, s, _ = x.shape
    tm = min(tm, s)
    full = lambda shape: pl.BlockSpec(shape, lambda bi, j: (0, 0), pipeline_mode=pl.Buffered(1))
    blk = pl.BlockSpec((1, tm, D_MODEL), lambda bi, j: (bi, j, 0))
    yblk = pl.BlockSpec((1, tm, GROUP), lambda bi, j: (bi, j, 0))
    return pl.pallas_call(
        _ffn_kernel,
        grid=(b, s // tm),
        in_specs=[blk] + [yblk] * 4 + [full((D_MODEL, D_MODEL)), full((1, D_MODEL)),
                                       full((1, D_MODEL)), full((D_MODEL, D_FF)), full((D_MODEL, D_FF)),
                                       full((FFN_CONV, D_FF)), full((1, D_FF)), full((D_FF, D_MODEL)),
                                       full((1, D_MODEL))],
        out_specs=blk,
        out_shape=jax.ShapeDtypeStruct((b, s, D_MODEL), F32),
        scratch_shapes=[pltpu.VMEM((8, D_FF), F32)],
        compiler_params=_params("arbitrary", "arbitrary"),
        name="out_proj_conv_glu_ffn",
    )(x, *ys, wo, go, g1, wg, wu, cw, cb, wd, g2)


def _row(v):
    return v.reshape(1, -1).astype(F32)


def _pad_rows(m, start, total=128):
    return jnp.zeros((total, m.shape[1]), m.dtype).at[start:start + m.shape[0]].set(m)


def _head_groups(w):
    d = w.shape[0]
    w4 = w.reshape(d, N_HEADS, HEAD_DIM)
    return jnp.concatenate([w4, jnp.zeros_like(w4)], axis=2).reshape(d, AUG)


def _value_groups(w):
    d = w.shape[0]
    w4 = w.reshape(d, N_HEADS, HEAD_DIM)
    pad = jnp.zeros((d, N_HEADS, V_ROWS - HEAD_DIM), w.dtype)
    return jnp.concatenate([w4, pad], axis=2).reshape(d, N_HEADS * V_ROWS)


def _bias_lane_constants():
    lanes = jnp.arange(AUG)
    head, local = lanes // 128, lanes % 128
    slope = jnp.exp2(-(ALIBI_MAX_EXP / N_HEADS) * (head + 1).astype(F32))
    at = lambda k: (local == AUG_LANE + k).astype(F32)
    zero = jnp.zeros((AUG,), F32)
    fox_q = -(at(0) + at(1) + at(2))
    c_parts = [p.astype(F32) for p in _split3(slope * LOG2E)]
    c_at = lambda base: sum(c_parts[k] * at(base + k) for k in range(3))
    ones_at = lambda base: at(base) + at(base + 1) + at(base + 2)
    rows = jnp.stack([jnp.full((AUG,), DIFF_QK ** -0.5 * LOG2E, F32), -ones_at(0), -ones_at(3),
                      c_at(6) + c_at(9), ones_at(6), ones_at(9), c_at(0) + c_at(3),
                      jnp.full((AUG,), HEAD_DIM ** -0.5 * LOG2E, F32), fox_q] + [zero] * 7)
    src = jnp.arange(N_SMALL)
    place = jnp.stack([((src[:, None] == 4 + head[None, :]) & (local[None, :] == AUG_LANE + k)).astype(BF16)
                       for k in range(3)])
    return rows, place


def kernel(x, norm_mix_pre, norm_mix_post, norm_ffn_pre, norm_ffn_post, w_in, w_out, diff_lambda_q1, diff_lambda_k1, diff_lambda_q2, diff_lambda_k2, diff_subln, ssm_conv_w, ssm_conv_b, ssm_dt_bias, ssm_a_log, ssm_d, ssm_norm, fox_f_bias, fox_norm, rwkv_mu, rwkv_w0, rwkv_w2, rwkv_a0, rwkv_a2, rwkv_g2, rwkv_k_k, rwkv_k_a, rwkv_r_k, rwkv_ln_w, rwkv_ln_b, ffn_w_gate, ffn_w_up, ffn_conv_w, ffn_conv_b, ffn_w_down):
    b, s, d = x.shape
    t = b * s
    depth = w_in.shape[0]
    tq = min(ATT_BLOCK, s)
    nq = s // tq
    o_dt = 768 + 256 + 768
    o_fq = o_dt + 4
    o_ff = o_fq + 768
    o_rp = o_ff + 4
    aug_rows, place = _bias_lane_constants()
    for l in range(depth):
        wl = w_in[l]
        w_main = jnp.concatenate(
            [_head_groups(wl[:, 256:512]), wl[:, 768:o_dt],
             _head_groups(wl[:, o_fq + 256:o_fq + 512]), wl[:, o_rp:]], axis=1).astype(BF16)
        w_t = jnp.concatenate(
            [_head_groups(wl[:, 0:256]), _value_groups(wl[:, 512:768]),
             _head_groups(wl[:, o_fq:o_fq + 256]), _value_groups(wl[:, o_fq + 512:o_ff])], axis=1).T.astype(BF16)
        w_small = jnp.concatenate([wl[:, o_dt:o_fq], wl[:, o_ff:o_rp],
                                   jnp.zeros((d, N_SMALL - 8), F32)], axis=1).astype(BF16)
        fbias = jnp.zeros((1, N_SMALL), F32).at[0, 4:8].set(fox_f_bias[l])
        (kd, u_ssm, kf, u_rwkv, u_small, qd_t, vd_t, qf_t, vf_t) = _in_proj(
            x.reshape(t, d), _row(norm_mix_pre[l]), w_main, w_t, w_small, aug_rows, fbias, place, b, s, tq)
        u_ssm = u_ssm.reshape(b, s, -1)
        u_rwkv = u_rwkv.reshape(b, s, -1)
        u_small = u_small.reshape(b, s, -1)

        lam_init = 0.8 - 0.6 * math.exp(-0.3 * l)
        g_diff = jnp.zeros((8, GROUP), F32).at[0].set(jnp.tile(diff_subln[l], N_HEADS)).at[1].set(lam_init)
        y_diff = _attention("diff", qd_t, kd.reshape(b, s, AUG), vd_t,
                            [_row(diff_lambda_q1[l]), _row(diff_lambda_k1[l]),
                             _row(diff_lambda_q2[l]), _row(diff_lambda_k2[l])], g_diff, tq)
        g_fox = jnp.zeros((8, GROUP), F32).at[0].set(jnp.tile(fox_norm[l], N_HEADS))
        y_fox = _attention("fox", qf_t, kf.reshape(b, s, AUG), vf_t, [], g_fox, tq)

        pad4 = lambda v: jnp.zeros((1, N_SMALL), F32).at[0, 0:4].set(v)
        y_ssm = _ssd(u_ssm, u_small, ssm_conv_w[l], _row(ssm_conv_b[l]), pad4(ssm_dt_bias[l]),
                     pad4(-jnp.exp(ssm_a_log[l])), _row(jnp.repeat(ssm_d[l], HEAD_DIM)), _row(ssm_norm[l]))

        y_rwkv = _rwkv(u_rwkv, _row(rwkv_mu[l]), _row(rwkv_w0[l]),
                       _pad_rows(rwkv_w2[l], 0).astype(BF16), _row(rwkv_a0[l]),
                       _pad_rows(rwkv_a2[l], 32).astype(BF16), _pad_rows(rwkv_g2[l], 64).astype(BF16),
                       _row(rwkv_k_k[l]), _row(rwkv_k_a[l]), _row(rwkv_r_k[l]),
                       _row(rwkv_ln_w[l]), _row(rwkv_ln_b[l]))

        x = _out_proj_ffn(x, [y_diff, y_ssm, y_fox, y_rwkv], w_out[l].astype(BF16), _row(norm_mix_post[l]),
                          _row(norm_ffn_pre[l]), ffn_w_gate[l].astype(BF16), ffn_w_up[l].astype(BF16),
                          ffn_conv_w[l], _row(ffn_conv_b[l]), ffn_w_down[l].astype(BF16),
                          _row(norm_ffn_post[l]))
    return x
```

```python
import functools
import math

import jax
import jax.numpy as jnp
from jax import lax
from jax.experimental import pallas as pl
from jax.experimental.pallas import tpu as pltpu

F32 = jnp.float32
BF16 = jnp.bfloat16

D_MODEL = 1024
GROUP = 256
N_HEADS = 4
HEAD_DIM = 64
DIFF_QK = 32
NORM_EPS = 1e-6
ALIBI_MAX_EXP = 8.0
SSM_STATE = 128
SSM_CONV = 4
SSM_CHUNK = 128
SSM_CHUNKS_PER_STEP = 2
RWKV_CHUNK = 64
RWKV_CHUNKS_PER_STEP = 4
RWKV_PROJ = 896
RWKV_LN_EPS = 64e-5
D_FF = 2816
FFN_CONV = 3
FFN_CHUNK = 1408
AUG = 2 * GROUP
AUG_LANE = HEAD_DIM
V_ROWS = HEAD_DIM + 16
LOG2E = 1.4426950408889634
N_T = 4 * GROUP
COLS = (0, GROUP, 5 * GROUP, 6 * GROUP, 6 * GROUP + RWKV_PROJ)
N_MAIN = COLS[-1]
N_SMALL = 128
ATT_BLOCK = 512
ATT_UNIT_WIDTH = 512
VMEM_LIMIT = 56 * 1024 * 1024


def _dot(a, b):
    return jnp.dot(a, b, preferred_element_type=F32)


def _dot_nt(a, b):
    return lax.dot_general(a, b, (((1,), (1,)), ((), ())), preferred_element_type=F32)


def _dot_tn(a, b):
    return lax.dot_general(a, b, (((0,), (0,)), ((), ())), preferred_element_type=F32)


def _split3(x):
    h1 = x.astype(BF16)
    r1 = x - h1.astype(F32)
    h2 = r1.astype(BF16)
    h3 = (r1 - h2.astype(F32)).astype(BF16)
    return h1, h2, h3


def _dot3_left(m_bf16, x):
    h1, h2, h3 = _split3(x)
    return _dot(m_bf16, h1) + _dot(m_bf16, h2) + _dot(m_bf16, h3)


def _dot3_right(x, m_bf16):
    h1, h2, h3 = _split3(x)
    return _dot(h1, m_bf16) + _dot(h2, m_bf16) + _dot(h3, m_bf16)


def _iota(shape, dim):
    return lax.broadcasted_iota(jnp.int32, shape, dim)


def _div(x, n):
    return lax.shift_right_logical(x, int(math.log2(n)))


def _mod(x, n):
    return lax.bitwise_and(x, n - 1)


def _softplus(x):
    return jnp.maximum(x, 0.0) + jnp.log(1.0 + jnp.exp(-jnp.abs(x)))


def _sigmoid(x):
    return 1.0 / (1.0 + jnp.exp(-x))


def _silu(x):
    return x * _sigmoid(x)


def _params(*sem):
    return pltpu.CompilerParams(dimension_semantics=sem, vmem_limit_bytes=VMEM_LIMIT)


def _in_proj_kernel(x_ref, g_ref, w_ref, wt_ref, ws_ref, aug_ref, augt_ref, fb_ref, place_ref,
                    kd_ref, os_ref, kf_ref, or_ref, osm_ref, qdt_ref, vdt_ref, qft_ref, vft_ref,
                    carry_ref, *, tm, tq, blocks_per_seq):
    i = pl.program_id(0)

    @pl.when(i % blocks_per_seq == 0)
    def _():
        carry_ref[...] = jnp.zeros(carry_ref.shape, F32)

    x = x_ref[...]
    ms = jnp.mean(x * x, axis=-1, keepdims=True)
    h = (x * lax.rsqrt(ms + NORM_EPS) * g_ref[...]).astype(BF16)
    proj = lambda n: _dot(h, w_ref[:, COLS[n]:COLS[n + 1]])
    proj_t = lambda n: _dot_nt(wt_ref[n * GROUP:(n + 1) * GROUP, :], h)
    col = lambda k: augt_ref[:, k:k + 1]
    pos = _mod(_iota((tm, 1), 0), tq)
    lo = _mod(pos, 256).astype(F32)
    hi = pos.astype(F32) - lo
    pos_t = _mod(_iota((1, tm), 1), tq)
    lo_t = _mod(pos_t, 256).astype(F32)
    hi_t = pos_t.astype(F32) - lo_t
    half = HEAD_DIM

    def head_groups_t(data, bias):
        return jnp.concatenate([part[hd * half:(hd + 1) * half] for hd in range(N_HEADS)
                                for part in (data, bias)], axis=0)

    def value_groups_t(data):
        ones = jnp.ones((V_ROWS - HEAD_DIM, tm), F32)
        return jnp.concatenate([part for hd in range(N_HEADS)
                                for part in (data[hd * half:(hd + 1) * half], ones)], axis=0)

    def head_groups(data, bias):
        low = _iota((1, 128), 1) < half
        out = []
        for pair in range(N_HEADS // 2):
            x2 = data[:, pair * 128:(pair + 1) * 128]
            out.append(jnp.where(low, x2, bias[:, 2 * pair * 128:(2 * pair + 1) * 128]))
            out.append(jnp.where(low, pltpu.roll(x2, half, 1), bias[:, (2 * pair + 1) * 128:(2 * pair + 2) * 128]))
        return jnp.concatenate(out, axis=1)

    qd_bias = col(1) * lo_t + col(2) * hi_t + col(3)
    qdt_ref[0] = head_groups_t(proj_t(0) * (DIFF_QK ** -0.5 * LOG2E), qd_bias).astype(BF16)
    vdt_ref[0, 0] = value_groups_t(proj_t(1)).astype(BF16)
    qft_ref[0] = head_groups_t(proj_t(2) * (HEAD_DIM ** -0.5 * LOG2E),
                               jnp.broadcast_to(col(8), (GROUP, tm))).astype(BF16)
    vft_ref[0, 0] = value_groups_t(proj_t(3)).astype(BF16)
    kd_bias = lo * aug_ref[4:5, :] + hi * aug_ref[5:6, :] + aug_ref[6:7, :]
    kd_ref[...] = head_groups(proj(0), kd_bias).astype(BF16)
    os_ref[...] = proj(1).astype(BF16)
    small = _dot(h, ws_ref[...])
    osm_ref[...] = small
    f = small + fb_ref[...]
    ls = jnp.minimum(f, 0.0) - jnp.log(1.0 + jnp.exp(-jnp.abs(f)))
    tri = (_iota((tm, tm), 0) >= _iota((tm, tm), 1)).astype(BF16)
    cum = _dot3_left(tri, ls) + carry_ref[0:1, :]
    carry_ref[0:1, :] = cum[tm - 1:tm, :]
    c1, c2, c3 = _split3(cum * LOG2E)
    kbias = _dot(c1, place_ref[0]) + _dot(c2, place_ref[1]) + _dot(c3, place_ref[2])
    kf_ref[...] = head_groups(proj(2), kbias).astype(BF16)
    or_ref[...] = proj(3).astype(BF16)


def _in_proj(x2, g, w_main, w_t, w_small, aug_rows, fbias, place, batch, seq, tq):
    t = x2.shape[0]
    tm = tq
    bps = seq // tm
    nv = N_HEADS * V_ROWS
    full = lambda shape: pl.BlockSpec(shape, lambda i: (0,) * len(shape))
    rows = lambda n: pl.BlockSpec((tm, n), lambda i: (i, 0))
    q_t = pl.BlockSpec((1, AUG, tm), lambda i: (i // bps, 0, i % bps))
    v_t = pl.BlockSpec((1, 1, nv, tm), lambda i: (i // bps, i % bps, 0, 0))
    widths = (AUG, 4 * GROUP, AUG, RWKV_PROJ)
    augt_bias = aug_rows.T.reshape(N_HEADS, 128, 16)[:, AUG_LANE:, :].reshape(GROUP, 16)
    q_shape = jax.ShapeDtypeStruct((batch, AUG, seq), BF16)
    v_shape = jax.ShapeDtypeStruct((batch, bps, nv, tm), BF16)
    return pl.pallas_call(
        functools.partial(_in_proj_kernel, tm=tm, tq=tq, blocks_per_seq=bps),
        grid=(t // tm,),
        in_specs=[rows(D_MODEL), full((1, D_MODEL)), full((D_MODEL, N_MAIN)), full((N_T, D_MODEL)),
                  full((D_MODEL, N_SMALL)), full((16, AUG)), full((GROUP, 16)),
                  full((1, N_SMALL)), full((3, N_SMALL, AUG))],
        out_specs=[rows(n) for n in widths] + [rows(N_SMALL), q_t, v_t, q_t, v_t],
        out_shape=[jax.ShapeDtypeStruct((t, n), BF16) for n in widths]
        + [jax.ShapeDtypeStruct((t, N_SMALL), F32), q_shape, v_shape, q_shape, v_shape],
        scratch_shapes=[pltpu.VMEM((8, N_SMALL), F32)],
        compiler_params=_params("arbitrary"),
        name="in_proj",
    )(x2, g, w_main, w_t, w_small, aug_rows, augt_bias, fbias, place)


def _head_mean_matrix():
    same = _div(_iota((GROUP, GROUP), 0), HEAD_DIM) == _div(_iota((GROUP, GROUP), 1), HEAD_DIM)
    return jnp.where(same, 1.0 / HEAD_DIM, 0.0).astype(BF16)


def _head_mean(x):
    return _dot(x.astype(BF16), _head_mean_matrix())


def _attn_kernel(*refs, kind, tq):
    if kind == "diff":
        (qt_ref, k_ref, vt_ref, lq1_ref, lk1_ref, lq2_ref, lk2_ref, g_ref, o_ref,
         qs_ref, m_ref, acc_ref) = refs
        n_c = 2
    else:
        qt_ref, k_ref, vt_ref, g_ref, o_ref, qs_ref, m_ref, acc_ref = refs
        n_c = 1
    tk = tq
    i = pl.program_id(1)

    sub = _iota((128, 1), 0)
    for h in range(N_HEADS):
        qh = qt_ref[0, h * 128:(h + 1) * 128, :]
        if kind == "diff":
            zero = jnp.zeros_like(qh)
            qs_ref[h, :, 0:tq] = jnp.where((sub >= DIFF_QK) & (sub < 2 * DIFF_QK), zero, qh)
            qs_ref[h, :, tq:2 * tq] = jnp.where(sub < DIFF_QK, zero, qh)
        else:
            qs_ref[h] = qh
    m_ref[...] = jnp.full(m_ref.shape, -jnp.inf, F32)
    acc_ref[...] = jnp.zeros(acc_ref.shape, F32)
    wu = min(ATT_UNIT_WIDTH, tq)
    units = [(h, l0) for h in range(N_HEADS) for l0 in range(0, n_c * tq, wu)]
    kpos = _iota((tk, wu), 0)
    qcol = _iota((tk, wu), 1)

    def run(kv_steps):
        work = [(j, masked, h, l0) for j, masked in kv_steps for h, l0 in units]

        def logits(unit):
            j, _, h, l0 = unit
            start = pl.multiple_of(j * tk, tk)
            return _dot(k_ref[0, pl.ds(start, tk), h * 128:(h + 1) * 128], qs_ref[h, :, l0:l0 + wu])

        def stats(unit, s):
            j, masked, h, l0 = unit
            slope2 = LOG2E * 2.0 ** (-(ALIBI_MAX_EXP / N_HEADS) * (h + 1))
            off = -slope2 * ((i - j) * tq).astype(F32) if kind == "diff" else 0.0
            if masked:
                s = jnp.where(kpos <= qcol + (l0 % tq), s, -jnp.inf)
            m_old = m_ref[h, :, l0:l0 + wu]
            m_new = jnp.maximum(m_old, jnp.max(s, axis=0, keepdims=True) + off)
            m_ref[h, :, l0:l0 + wu] = m_new
            return s, jnp.exp2(m_old - m_new), m_new - off

        def accumulate(unit, s, alpha, shift):
            j, _, h, l0 = unit
            p = jnp.exp2(s - shift).astype(BF16)
            v_h = vt_ref[0, j, h * V_ROWS:(h + 1) * V_ROWS, :]
            acc_ref[h, :, l0:l0 + wu] = acc_ref[h, :, l0:l0 + wu] * alpha + _dot(v_h, p)

        n_work = len(work)
        s_of = {0: logits(work[0])}
        if n_work > 1:
            s_of[1] = logits(work[1])
        st_of = {0: stats(work[0], s_of.pop(0))}
        for n in range(n_work):
            if n + 2 < n_work:
                s_of[n + 2] = logits(work[n + 2])
            if n + 1 < n_work:
                st_of[n + 1] = stats(work[n + 1], s_of.pop(n + 1))
            accumulate(work[n], *st_of.pop(n))

    def body(t, carry):
        run([(2 * t, False), (2 * t + 1, False)])
        return carry

    lax.fori_loop(0, i // 2, body, 0)

    @pl.when(i % 2 == 1)
    def _():
        run([(i - 1, False), (i, True)])

    @pl.when(i % 2 == 0)
    def _():
        run([(i, True)])

    parts = []
    if kind == "diff":
        lam_init = g_ref[1:2, 0:1]
        lam = (jnp.exp(jnp.sum(lq1_ref[...] * lk1_ref[...], axis=1, keepdims=True))
               - jnp.exp(jnp.sum(lq2_ref[...] * lk2_ref[...], axis=1, keepdims=True)) + lam_init)
        for h in range(N_HEADS):
            on = acc_ref[h, 0:HEAD_DIM, :] * (1.0 / acc_ref[h, HEAD_DIM:HEAD_DIM + 1, :])
            parts.append(on[:, 0:tq] - lam * on[:, tq:2 * tq])
    else:
        for h in range(N_HEADS):
            parts.append(acc_ref[h, 0:HEAD_DIM, :] * (1.0 / acc_ref[h, HEAD_DIM:HEAD_DIM + 1, :]))
    o = jnp.concatenate(parts, axis=0).T
    ms = _head_mean(o * o)
    y = o * lax.rsqrt(ms + NORM_EPS) * g_ref[0:1, :]
    if kind == "diff":
        y = y * (1.0 - lam_init)
    o_ref[0] = y.astype(o_ref.dtype)


def _attention(kind, q_t, k_aug, v_t, extra, gains, tq):
    b, s, _ = k_aug.shape
    nq = s // tq
    w = (2 if kind == "diff" else 1) * tq
    extra_specs = [pl.BlockSpec((1, DIFF_QK), lambda bi, i: (0, 0))] * len(extra)
    once = pl.Buffered(1)
    return pl.pallas_call(
        functools.partial(_attn_kernel, kind=kind, tq=tq),
        grid=(b, nq),
        in_specs=[pl.BlockSpec((1, AUG, tq), lambda bi, i: (bi, 0, i)),
                  pl.BlockSpec((1, s, AUG), lambda bi, i: (bi, 0, 0), pipeline_mode=once),
                  pl.BlockSpec((1, nq, N_HEADS * V_ROWS, tq), lambda bi, i: (bi, 0, 0, 0), pipeline_mode=once)]
        + extra_specs + [pl.BlockSpec((8, GROUP), lambda bi, i: (0, 0))],
        out_specs=pl.BlockSpec((1, tq, GROUP), lambda bi, i: (bi, i, 0)),
        out_shape=jax.ShapeDtypeStruct((b, s, GROUP), BF16),
        scratch_shapes=[pltpu.VMEM((N_HEADS, 128, w), BF16), pltpu.VMEM((N_HEADS, 1, w), F32),
                        pltpu.VMEM((N_HEADS, V_ROWS, w), F32)],
        compiler_params=_params("arbitrary", "arbitrary"),
        name=kind + "_attention",
    )(q_t, k_aug, v_t, *extra, gains)


def _ssd_kernel(u_ref, sm_ref, cw_ref, cb_ref, dtb_ref, a_ref, d_ref, g_ref, o_ref,
                prev_ref, state_ref):
    L = SSM_CHUNK
    c = pl.program_id(1)

    @pl.when(c == 0)
    def _():
        prev_ref[...] = jnp.zeros(prev_ref.shape, F32)
        state_ref[...] = jnp.zeros(state_ref.shape, F32)

    T = u_ref.shape[1]
    chunks = range(T // L)
    u = u_ref[0].astype(F32)
    z = u[:, 0:GROUP]
    raw = u[:, GROUP:]
    tail = prev_ref[...]
    row8 = _iota((8, 1), 0)
    xbc = raw * cw_ref[SSM_CONV - 1:SSM_CONV, :] + cb_ref[...]
    for d in range(1, SSM_CONV):
        rolled = pltpu.roll(raw, d, 0)
        head = jnp.where(row8 < d, pltpu.roll(tail, d, 0), rolled[0:8])
        shifted = jnp.concatenate([head, rolled[8:]], axis=0)
        xbc = xbc + shifted * cw_ref[SSM_CONV - 1 - d:SSM_CONV - d, :]
    prev_ref[...] = raw[T - 8:T, :]
    xbc = _silu(xbc)
    xs = xbc[:, 0:GROUP]
    bm = xbc[:, GROUP:2 * GROUP]
    cm = xbc[:, 2 * GROUP:3 * GROUP]

    dt = _softplus(sm_ref[0] + dtb_ref[...])
    da = dt * a_ref[...]
    rep = (_iota((128, 4 * 128), 0) == _div(_iota((128, 4 * 128), 1), 128)).astype(BF16)
    exp_h = (_iota((128, GROUP), 0) == _div(_iota((128, GROUP), 1), HEAD_DIM)).astype(BF16)
    dt_e = _dot3_right(dt, exp_h)
    ti = _iota((T, T), 0)
    tj = _iota((T, T), 1)
    tri = ((ti >= tj) & (_div(ti, L) == _div(tj, L))).astype(BF16)
    acs = _dot3_right(_dot3_left(tri, da), rep)
    causal = _iota((L, L), 0) >= _iota((L, L), 1)
    xdt = (xs * dt_e).astype(BF16)
    hl = _div(_iota((1, GROUP), 1), HEAD_DIM)
    blk = _div(_iota((4 * 128, GROUP), 0), 128) == _div(_iota((4 * 128, GROUP), 1), HEAD_DIM)

    last, y_diag, cdc, st_new, chunk_decay = [], [], [], [], []
    for n in chunks:
        rows = slice(n * L, (n + 1) * L)
        last_n = acs[n * L + L - 1:n * L + L, :]
        scores, bd, cd = [], [], []
        for h in range(N_HEADS):
            g = h // 2
            col = acs[rows, h * 128:(h + 1) * 128]
            b_g = bm[rows, g * 128:(g + 1) * 128]
            c_g = cm[rows, g * 128:(g + 1) * 128]
            if h % 2 == 0:
                cb = _dot_nt(c_g.astype(BF16), b_g.astype(BF16))
            decay = jnp.exp(jnp.where(causal, col - col.T, -jnp.inf))
            scores.append((cb * decay).astype(BF16))
            bd.append((b_g * jnp.exp(last_n[:, h * 128:(h + 1) * 128] - col)).astype(BF16))
            cd.append((c_g * jnp.exp(col)).astype(BF16))
        y_st = _dot(jnp.concatenate(scores, axis=0), xdt[rows])
        y_n = jnp.zeros((L, GROUP), F32)
        for h in range(N_HEADS):
            y_n = y_n + jnp.where(hl == h, y_st[h * L:(h + 1) * L, :], 0.0)
        y_diag.append(y_n)
        cdc.append(jnp.concatenate(cd, axis=1))
        st_new.append(jnp.where(blk, _dot_tn(jnp.concatenate(bd, axis=1), xdt[rows]), 0.0))
        chunk_decay.append(jnp.concatenate(
            [jnp.broadcast_to(jnp.exp(last_n[:, h * 128:h * 128 + 1]), (128, GROUP)) for h in range(N_HEADS)],
            axis=0))
    state = state_ref[...]
    ys = []
    for n in chunks:
        ys.append(y_diag[n] + _dot(cdc[n], state.astype(BF16)))
        state = state * chunk_decay[n] + st_new[n]
    state_ref[...] = state
    y = jnp.concatenate(ys, axis=0)

    y = y + xs * d_ref[...]
    yg = y * _silu(z)
    sq = yg * yg
    half = GROUP // 2
    ms0 = jnp.mean(sq[:, 0:half], axis=1, keepdims=True)
    ms1 = jnp.mean(sq[:, half:], axis=1, keepdims=True)
    ms = jnp.where(_iota((1, GROUP), 1) < half, ms0, ms1)
    o_ref[0] = (yg * lax.rsqrt(ms + NORM_EPS) * g_ref[...]).astype(o_ref.dtype)


def _ssd(u, small, conv_w, conv_b, dt_bias, a_neg, d_e, norm_g):
    b, s, _ = u.shape
    L = SSM_CHUNK * SSM_CHUNKS_PER_STEP
    cdim = 3 * GROUP
    full = lambda shape: pl.BlockSpec(shape, lambda bi, c: (0,) * len(shape))
    return pl.pallas_call(
        _ssd_kernel,
        grid=(b, s // L),
        in_specs=[pl.BlockSpec((1, L, 4 * GROUP), lambda bi, c: (bi, c, 0)),
                  pl.BlockSpec((1, L, N_SMALL), lambda bi, c: (bi, c, 0)),
                  full((SSM_CONV, cdim)), full((1, cdim)), full((1, N_SMALL)), full((1, N_SMALL)),
                  full((1, GROUP)), full((1, GROUP))],
        out_specs=pl.BlockSpec((1, L, GROUP), lambda bi, c: (bi, c, 0)),
        out_shape=jax.ShapeDtypeStruct((b, s, GROUP), BF16),
        scratch_shapes=[pltpu.VMEM((8, cdim), F32), pltpu.VMEM((4 * SSM_STATE, GROUP), F32)],
        compiler_params=_params("arbitrary", "arbitrary"),
        name="ssd_mixer",
    )(u, small, conv_w, conv_b, dt_bias, a_neg, d_e, norm_g)


def _rwkv_kernel(p_ref, mu_ref, w0_ref, w2_ref, a0_ref, a2_ref, g2_ref, kk_ref, ka_ref, rk_ref,
                 lnw_ref, lnb_ref, o_ref, last_ref, h_ref):
    C = RWKV_CHUNK
    R = N_HEADS * C
    T = p_ref.shape[1]
    n_chunks = T // C
    c = pl.program_id(1)

    @pl.when(c == 0)
    def _():
        last_ref[...] = jnp.zeros(last_ref.shape, F32)
        h_ref[...] = jnp.zeros(h_ref.shape, F32)

    p = p_ref[0].astype(F32)
    row = _iota((T, 1), 0)
    p_prev = jnp.where(row >= 1, pltpu.roll(p, 1, 0), jnp.broadcast_to(last_ref[0:1, :], p.shape))
    last_ref[0:1, :] = p[T - 1:T, :]
    pm = p + (p_prev - p) * mu_ref[...]
    r = pm[:, 0:GROUP]
    k = pm[:, GROUP:2 * GROUP]
    v = pm[:, 2 * GROUP:3 * GROUP]
    xlow = pm[:, 3 * GROUP:]
    w_raw = -_softplus(-(w0_ref[...] + _dot(jnp.tanh(xlow).astype(BF16), w2_ref[...]))) - 0.5
    logw = -jnp.exp(w_raw)
    a = _sigmoid(a0_ref[...] + _dot(xlow.astype(BF16), a2_ref[...]))
    gate = _dot(_sigmoid(xlow).astype(BF16), g2_ref[...])
    kk = k * kk_ref[...]
    kk = kk * lax.rsqrt(_head_mean(kk * kk) * HEAD_DIM + 1e-12)
    k = k * (1.0 + (a - 1.0) * ka_ref[...])
    bonus = (_head_mean(r * k * rk_ref[...]) * HEAD_DIM) * v

    ti = _iota((T, T), 0)
    tj = _iota((T, T), 1)
    tri = ((ti >= tj) & (_div(ti, C) == _div(tj, C))).astype(BF16)
    lc = _dot3_left(tri, logw)
    e_pos = jnp.exp(lc)
    e_neg = jnp.exp(-lc)
    a_t = -kk * jnp.exp(lc - logw)
    b_t = (a * kk) * e_neg
    k_t = k * e_neg
    r_t = r * e_pos

    hl = _div(_iota((1, GROUP), 1), HEAD_DIM)
    ri = _iota((R, R), 0)
    ci = _iota((R, R), 1)
    same = _div(ri, C) == _div(ci, C)
    strict = same & (_mod(ri, C) > _mod(ci, C))
    incl = same & (_mod(ri, C) >= _mod(ci, C))
    eye = ri == ci
    chunks = range(n_chunks)

    def stack(x, n):
        xc = x[n * C:(n + 1) * C]
        return jnp.concatenate([jnp.where(hl == h, xc, 0.0) for h in range(N_HEADS)], axis=0)

    a_s = [stack(a_t, n) for n in chunks]
    b_s = [stack(b_t, n) for n in chunks]
    k_s = [stack(k_t, n) for n in chunks]
    r_s = [stack(r_t, n) for n in chunks]
    v_sb = [stack(v, n).astype(BF16) for n in chunks]
    pc = [e_pos[n * C + C - 1:n * C + C, :] for n in chunks]
    m_all = [_dot_nt(jnp.concatenate([a_s[n], r_s[n]], axis=0).astype(BF16),
                     jnp.concatenate([b_s[n], k_s[n]], axis=0).astype(BF16)) for n in chunks]
    l_p = [jnp.where(strict, m_all[n][0:R, 0:R], 0.0) for n in chunks]
    m_ak = [jnp.where(strict, m_all[n][0:R, R:], 0.0).astype(BF16) for n in chunks]
    m_rb = [jnp.where(incl, m_all[n][R:, 0:R], 0.0).astype(BF16) for n in chunks]
    m_rk = [jnp.where(incl, m_all[n][R:, R:], 0.0).astype(BF16) for n in chunks]
    t_m = [jnp.where(eye, 1.0, 0.0) + l_p[n] for n in chunks]
    for _ in range(int(math.log2(C)) - 1):
        l_pb = [l_p[n].astype(BF16) for n in chunks]
        l_p = [_dot(l_pb[n], l_pb[n]) for n in chunks]
        t_m = [t_m[n] + _dot(t_m[n].astype(BF16), l_p[n].astype(BF16)) for n in chunks]
    t_b = [t_m[n].astype(BF16) for n in chunks]
    mv = [_dot(m_ak[n], v_sb[n]).astype(BF16) for n in chunks]
    w_sb = [_dot(t_b[n], a_s[n].astype(BF16)).astype(BF16) for n in chunks]
    u0_b = [_dot(t_b[n], mv[n]).astype(BF16) for n in chunks]
    bh = [(b_s[n] * pc[n]).astype(BF16) for n in chunks]
    kh = [(k_s[n] * pc[n]).astype(BF16) for n in chunks]
    g_m = [(jnp.where(eye, jnp.broadcast_to(pc[n], (R, R)), 0.0) + _dot_tn(bh[n], w_sb[n])).astype(BF16)
           for n in chunks]
    d_m = [_dot_tn(bh[n], u0_b[n]) + _dot_tn(kh[n], v_sb[n]) for n in chunks]
    q_m = [(r_s[n] + _dot(m_rb[n], w_sb[n])).astype(BF16) for n in chunks]
    y0 = [_dot(m_rb[n], u0_b[n]) + _dot(m_rk[n], v_sb[n]) for n in chunks]
    state = h_ref[...]
    ys = []
    for n in chunks:
        h_b = state.astype(BF16)
        y_s = _dot(q_m[n], h_b) + y0[n]
        state = _dot(g_m[n], h_b) + d_m[n]
        ys.append(y_s[0:C] + y_s[C:2 * C] + y_s[2 * C:3 * C] + y_s[3 * C:4 * C])
    h_ref[...] = state
    y = jnp.concatenate(ys, axis=0)

    mean = _head_mean(y)
    yc = y - mean
    var = _head_mean(yc * yc)
    o = yc * lax.rsqrt(var + RWKV_LN_EPS) * lnw_ref[...] + lnb_ref[...] + bonus
    o_ref[0] = (o * gate).astype(o_ref.dtype)


def _rwkv(p, mu, w0, w2p, a0, a2p, g2p, k_k, k_a, r_k, ln_w, ln_b):
    b, s, _ = p.shape
    C = RWKV_CHUNK * RWKV_CHUNKS_PER_STEP
    full = lambda shape: pl.BlockSpec(shape, lambda bi, c: (0,) * len(shape))
    vec = full((1, GROUP))
    mat = full((128, GROUP))
    return pl.pallas_call(
        _rwkv_kernel,
        grid=(b, s // C),
        in_specs=[pl.BlockSpec((1, C, RWKV_PROJ), lambda bi, c: (bi, c, 0)), full((1, RWKV_PROJ)),
                  vec, mat, vec, mat, mat, vec, vec, vec, vec, vec],
        out_specs=pl.BlockSpec((1, C, GROUP), lambda bi, c: (bi, c, 0)),
        out_shape=jax.ShapeDtypeStruct((b, s, GROUP), BF16),
        scratch_shapes=[pltpu.VMEM((8, RWKV_PROJ), F32), pltpu.VMEM((GROUP, GROUP), F32)],
        compiler_params=_params("arbitrary", "arbitrary"),
        name="rwkv7_mix",
    )(p, mu, w0, w2p, a0, a2p, g2p, k_k, k_a, r_k, ln_w, ln_b)


def _ffn_kernel(x_ref, y0_ref, y1_ref, y2_ref, y3_ref, wo_ref, go_ref,
                g1_ref, wg_ref, wu_ref, cw_ref, cb_ref, wd_ref, g2_ref, o_ref, tail_ref):
    tm = x_ref.shape[1]
    j = pl.program_id(1)

    @pl.when(j == 0)
    def _():
        tail_ref[...] = jnp.zeros(tail_ref.shape, F32)

    mix = _dot(y0_ref[0], wo_ref[0:GROUP, :])
    mix = mix + _dot(y1_ref[0], wo_ref[GROUP:2 * GROUP, :])
    mix = mix + _dot(y2_ref[0], wo_ref[2 * GROUP:3 * GROUP, :])
    mix = mix + _dot(y3_ref[0], wo_ref[3 * GROUP:4 * GROUP, :])
    ms0 = jnp.mean(mix * mix, axis=-1, keepdims=True)
    x = x_ref[0] + mix * lax.rsqrt(ms0 + NORM_EPS) * go_ref[...]
    ms = jnp.mean(x * x, axis=-1, keepdims=True)
    h = (x * lax.rsqrt(ms + NORM_EPS) * g1_ref[...]).astype(BF16)
    row8 = _iota((8, 1), 0)
    acc = jnp.zeros((tm, D_MODEL), F32)
    for c in range(D_FF // FFN_CHUNK):
        cs = slice(c * FFN_CHUNK, (c + 1) * FFN_CHUNK)
        gate = _dot(h, wg_ref[:, cs])
        up = _dot(h, wu_ref[:, cs])
        tail = tail_ref[:, cs]
        conv = gate * cw_ref[FFN_CONV - 1:FFN_CONV, cs] + cb_ref[:, cs]
        for d in range(1, FFN_CONV):
            rolled = pltpu.roll(gate, d, 0)
            head = jnp.where(row8 < d, pltpu.roll(tail, d, 0), rolled[0:8])
            shifted = jnp.concatenate([head, rolled[8:]], axis=0)
            conv = conv + shifted * cw_ref[FFN_CONV - 1 - d:FFN_CONV - d, cs]
        tail_ref[:, cs] = gate[tm - 8:tm, :]
        inner = 0.7978845608028654 * (conv + 0.044715 * conv * conv * conv)
        act = 0.5 * conv * (1.0 + jnp.tanh(inner))
        acc = acc + _dot((act * up).astype(BF16), wd_ref[cs, :])
    ms2 = jnp.mean(acc * acc, axis=-1, keepdims=True)
    o_ref[0] = x + acc * lax.rsqrt(ms2 + NORM_EPS) * g2_ref[...]


def _out_proj_ffn(x, ys, wo, go, g1, wg, wu, cw, cb, wd, g2, tm=512):
    b, s, _ = x.shape
    tm = min(tm, s)
    full = lambda shape: pl.BlockSpec(shape, lambda bi, j: (0, 0), pipeline_mode=pl.Buffered(1))
    blk = pl.BlockSpec((1, tm, D_MODEL), lambda bi, j: (bi, j, 0))
    yblk = pl.BlockSpec((1, tm, GROUP), lambda bi, j: (bi, j, 0))
    return pl.pallas_call(
        _ffn_kernel,
        grid=(b, s // tm),
        in_specs=[blk] + [yblk] * 4 + [full((D_MODEL, D_MODEL)), full((1, D_MODEL)),
                                       full((1, D_MODEL)), full((D_MODEL, D_FF)), full((D_MODEL, D_FF)),
                                       full((FFN_CONV, D_FF)), full((1, D_FF)), full((D_FF, D_MODEL)),
                                       full((1, D_MODEL))],
        out_specs=blk,
        out_shape=jax.ShapeDtypeStruct((b, s, D_MODEL), F32),
        scratch_shapes=[pltpu.VMEM((8, D_FF), F32)],
        compiler_params=_params("arbitrary", "arbitrary"),
        name="out_proj_conv_glu_ffn",
    )(x, *ys, wo, go, g1, wg, wu, cw, cb, wd, g2)


def _row(v):
    return v.reshape(1, -1).astype(F32)


def _pad_rows(m, start, total=128):
    return jnp.zeros((total, m.shape[1]), m.dtype).at[start:start + m.shape[0]].set(m)


def _bias_lane_constants():
    lanes = jnp.arange(AUG)
    head, local = lanes // 128, lanes % 128
    slope = jnp.exp2(-(ALIBI_MAX_EXP / N_HEADS) * (head + 1).astype(F32))
    at = lambda k: (local == AUG_LANE + k).astype(F32)
    zero = jnp.zeros((AUG,), F32)
    fox_q = -(at(0) + at(1) + at(2))
    c_parts = [p.astype(F32) for p in _split3(slope * LOG2E)]
    c_at = lambda base: sum(c_parts[k] * at(base + k) for k in range(3))
    ones_at = lambda base: at(base) + at(base + 1) + at(base + 2)
    rows = jnp.stack([jnp.full((AUG,), DIFF_QK ** -0.5 * LOG2E, F32), -ones_at(0), -ones_at(3),
                      c_at(6) + c_at(9), ones_at(6), ones_at(9), c_at(0) + c_at(3),
                      jnp.full((AUG,), HEAD_DIM ** -0.5 * LOG2E, F32), fox_q] + [zero] * 7)
    src = jnp.arange(N_SMALL)
    place = jnp.stack([((src[:, None] == 4 + head[None, :]) & (local[None, :] == AUG_LANE + k)).astype(BF16)
                       for k in range(3)])
    return rows, place


def kernel(x, norm_mix_pre, norm_mix_post, norm_ffn_pre, norm_ffn_post, w_in, w_out, diff_lambda_q1, diff_lambda_k1, diff_lambda_q2, diff_lambda_k2, diff_subln, ssm_conv_w, ssm_conv_b, ssm_dt_bias, ssm_a_log, ssm_d, ssm_norm, fox_f_bias, fox_norm, rwkv_mu, rwkv_w0, rwkv_w2, rwkv_a0, rwkv_a2, rwkv_g2, rwkv_k_k, rwkv_k_a, rwkv_r_k, rwkv_ln_w, rwkv_ln_b, ffn_w_gate, ffn_w_up, ffn_conv_w, ffn_conv_b, ffn_w_down):
    b, s, d = x.shape
    t = b * s
    depth = w_in.shape[0]
    tq = min(ATT_BLOCK, s)
    nq = s // tq
    o_dt = 768 + 256 + 768
    o_fq = o_dt + 4
    o_ff = o_fq + 768
    o_rp = o_ff + 4
    aug_rows, place = _bias_lane_constants()
    for l in range(depth):
        wl = w_in[l]
        w_main = jnp.concatenate(
            [wl[:, 256:512], wl[:, 768:o_dt], wl[:, o_fq + 256:o_fq + 512], wl[:, o_rp:]], axis=1).astype(BF16)
        w_t = jnp.concatenate(
            [wl[:, 0:256], wl[:, 512:768], wl[:, o_fq:o_fq + 256], wl[:, o_fq + 512:o_ff]], axis=1).T.astype(BF16)
        w_small = jnp.concatenate([wl[:, o_dt:o_fq], wl[:, o_ff:o_rp],
                                   jnp.zeros((d, N_SMALL - 8), F32)], axis=1).astype(BF16)
        fbias = jnp.zeros((1, N_SMALL), F32).at[0, 4:8].set(fox_f_bias[l])
        (kd, u_ssm, kf, u_rwkv, u_small, qd_t, vd_t, qf_t, vf_t) = _in_proj(
            x.reshape(t, d), _row(norm_mix_pre[l]), w_main, w_t, w_small, aug_rows, fbias, place, b, s, tq)
        u_ssm = u_ssm.reshape(b, s, -1)
        u_rwkv = u_rwkv.reshape(b, s, -1)
        u_small = u_small.reshape(b, s, -1)

        lam_init = 0.8 - 0.6 * math.exp(-0.3 * l)
        g_diff = jnp.zeros((8, GROUP), F32).at[0].set(jnp.tile(diff_subln[l], N_HEADS)).at[1].set(lam_init)
        y_diff = _attention("diff", qd_t, kd.reshape(b, s, AUG), vd_t,
                            [_row(diff_lambda_q1[l]), _row(diff_lambda_k1[l]),
                             _row(diff_lambda_q2[l]), _row(diff_lambda_k2[l])], g_diff, tq)
        g_fox = jnp.zeros((8, GROUP), F32).at[0].set(jnp.tile(fox_norm[l], N_HEADS))
        y_fox = _attention("fox", qf_t, kf.reshape(b, s, AUG), vf_t, [], g_fox, tq)

        pad4 = lambda v: jnp.zeros((1, N_SMALL), F32).at[0, 0:4].set(v)
        y_ssm = _ssd(u_ssm, u_small, ssm_conv_w[l], _row(ssm_conv_b[l]), pad4(ssm_dt_bias[l]),
                     pad4(-jnp.exp(ssm_a_log[l])), _row(jnp.repeat(ssm_d[l], HEAD_DIM)), _row(ssm_norm[l]))

        y_rwkv = _rwkv(u_rwkv, _row(rwkv_mu[l]), _row(rwkv_w0[l]),
                       _pad_rows(rwkv_w2[l], 0).astype(BF16), _row(rwkv_a0[l]),
                       _pad_rows(rwkv_a2[l], 32).astype(BF16), _pad_rows(rwkv_g2[l], 64).astype(BF16),
                       _row(rwkv_k_k[l]), _row(rwkv_k_a[l]), _row(rwkv_r_k[l]),
                       _row(rwkv_ln_w[l]), _row(rwkv_ln_b[l]))

        x = _out_proj_ffn(x, [y_diff, y_ssm, y_fox, y_rwkv], w_out[l].astype(BF16), _row(norm_mix_post[l]),
                          _row(norm_ffn_pre[l]), ffn_w_gate[l].astype(BF16), ffn_w_up[l].astype(BF16),
                          ffn_conv_w[l], _row(ffn_conv_b[l]), ffn_w_down[l].astype(BF16),
                          _row(norm_ffn_post[l]))
    return x
```

```python
import functools
import math

import jax
import jax.numpy as jnp
from jax import lax
from jax.experimental import pallas as pl
from jax.experimental.pallas import tpu as pltpu

F32 = jnp.float32
BF16 = jnp.bfloat16

D_MODEL = 1024
GROUP = 256
N_HEADS = 4
HEAD_DIM = 64
DIFF_QK = 32
NORM_EPS = 1e-6
ALIBI_MAX_EXP = 8.0
SSM_STATE = 128
SSM_CONV = 4
SSM_CHUNK = 128
SSM_CHUNKS_PER_STEP = 2
RWKV_CHUNK = 64
RWKV_CHUNKS_PER_STEP = 4
RWKV_PROJ = 896
RWKV_LN_EPS = 64e-5
D_FF = 2816
FFN_CONV = 3
FFN_CHUNK = 2816
AUG = 2 * GROUP
AUG_LANE = HEAD_DIM
V_ROWS = HEAD_DIM + 16
LOG2E = 1.4426950408889634
N_T = 4 * GROUP
COLS = (0, GROUP, 5 * GROUP, 6 * GROUP, 6 * GROUP + RWKV_PROJ)
N_MAIN = COLS[-1]
N_SMALL = 128
ATT_BLOCK = 512
ATT_UNIT_WIDTH = 512
VMEM_LIMIT = 56 * 1024 * 1024


def _dot(a, b):
    return jnp.dot(a, b, preferred_element_type=F32)


def _dot_nt(a, b):
    return lax.dot_general(a, b, (((1,), (1,)), ((), ())), preferred_element_type=F32)


def _dot_tn(a, b):
    return lax.dot_general(a, b, (((0,), (0,)), ((), ())), preferred_element_type=F32)


def _split3(x):
    h1 = x.astype(BF16)
    r1 = x - h1.astype(F32)
    h2 = r1.astype(BF16)
    h3 = (r1 - h2.astype(F32)).astype(BF16)
    return h1, h2, h3


def _dot3_left(m_bf16, x):
    h1, h2, h3 = _split3(x)
    return _dot(m_bf16, h1) + _dot(m_bf16, h2) + _dot(m_bf16, h3)


def _dot3_right(x, m_bf16):
    h1, h2, h3 = _split3(x)
    return _dot(h1, m_bf16) + _dot(h2, m_bf16) + _dot(h3, m_bf16)


def _iota(shape, dim):
    return lax.broadcasted_iota(jnp.int32, shape, dim)


def _div(x, n):
    return lax.shift_right_logical(x, int(math.log2(n)))


def _mod(x, n):
    return lax.bitwise_and(x, n - 1)


def _softplus(x):
    return jnp.maximum(x, 0.0) + jnp.log(1.0 + jnp.exp(-jnp.abs(x)))


def _sigmoid(x):
    return 1.0 / (1.0 + jnp.exp(-x))


def _silu(x):
    return x * _sigmoid(x)


def _params(*sem):
    return pltpu.CompilerParams(dimension_semantics=sem, vmem_limit_bytes=VMEM_LIMIT)


def _in_proj_kernel(x_ref, g_ref, w_ref, wt_ref, ws_ref, aug_ref, augt_ref, fb_ref, place_ref,
                    kd_ref, os_ref, kf_ref, or_ref, osm_ref, qdt_ref, vdt_ref, qft_ref, vft_ref,
                    carry_ref, *, tm, tq, blocks_per_seq):
    i = pl.program_id(0)

    @pl.when(i % blocks_per_seq == 0)
    def _():
        carry_ref[...] = jnp.zeros(carry_ref.shape, F32)

    x = x_ref[...]
    ms = jnp.mean(x * x, axis=-1, keepdims=True)
    h = (x * lax.rsqrt(ms + NORM_EPS) * g_ref[...]).astype(BF16)
    proj = lambda n: _dot(h, w_ref[:, COLS[n]:COLS[n + 1]])
    proj_t = lambda n: _dot_nt(wt_ref[n * GROUP:(n + 1) * GROUP, :], h)
    col = lambda k: augt_ref[:, k:k + 1]
    pos = _mod(_iota((tm, 1), 0), tq)
    lo = _mod(pos, 256).astype(F32)
    hi = pos.astype(F32) - lo
    pos_t = _mod(_iota((1, tm), 1), tq)
    lo_t = _mod(pos_t, 256).astype(F32)
    hi_t = pos_t.astype(F32) - lo_t
    half = HEAD_DIM

    def head_groups_t(data, bias):
        return jnp.concatenate([part[hd * half:(hd + 1) * half] for hd in range(N_HEADS)
                                for part in (data, bias)], axis=0)

    def value_groups_t(data):
        ones = jnp.ones((V_ROWS - HEAD_DIM, tm), F32)
        return jnp.concatenate([part for hd in range(N_HEADS)
                                for part in (data[hd * half:(hd + 1) * half], ones)], axis=0)

    def head_groups(data, bias):
        low = _iota((1, 128), 1) < half
        out = []
        for pair in range(N_HEADS // 2):
            x2 = data[:, pair * 128:(pair + 1) * 128]
            out.append(jnp.where(low, x2, bias[:, 2 * pair * 128:(2 * pair + 1) * 128]))
            out.append(jnp.where(low, pltpu.roll(x2, half, 1), bias[:, (2 * pair + 1) * 128:(2 * pair + 2) * 128]))
        return jnp.concatenate(out, axis=1)

    qd_bias = col(1) * lo_t + col(2) * hi_t + col(3)
    qdt_ref[0] = head_groups_t(proj_t(0) * (DIFF_QK ** -0.5 * LOG2E), qd_bias).astype(BF16)
    vdt_ref[0, 0] = value_groups_t(proj_t(1)).astype(BF16)
    qft_ref[0] = head_groups_t(proj_t(2) * (HEAD_DIM ** -0.5 * LOG2E),
                               jnp.broadcast_to(col(8), (GROUP, tm))).astype(BF16)
    vft_ref[0, 0] = value_groups_t(proj_t(3)).astype(BF16)
    kd_bias = lo * aug_ref[4:5, :] + hi * aug_ref[5:6, :] + aug_ref[6:7, :]
    kd_ref[...] = head_groups(proj(0), kd_bias).astype(BF16)
    os_ref[...] = proj(1).astype(BF16)
    small = _dot(h, ws_ref[...])
    osm_ref[...] = small
    f = small + fb_ref[...]
    ls = jnp.minimum(f, 0.0) - jnp.log(1.0 + jnp.exp(-jnp.abs(f)))
    tri = (_iota((tm, tm), 0) >= _iota((tm, tm), 1)).astype(BF16)
    cum = _dot3_left(tri, ls) + carry_ref[0:1, :]
    carry_ref[0:1, :] = cum[tm - 1:tm, :]
    c1, c2, c3 = _split3(cum * LOG2E)
    kbias = _dot(c1, place_ref[0]) + _dot(c2, place_ref[1]) + _dot(c3, place_ref[2])
    kf_ref[...] = head_groups(proj(2), kbias).astype(BF16)
    or_ref[...] = proj(3).astype(BF16)


def _in_proj(x2, g, w_main, w_t, w_small, aug_rows, fbias, place, batch, seq, tq):
    t = x2.shape[0]
    tm = tq
    bps = seq // tm
    nv = N_HEADS * V_ROWS
    full = lambda shape: pl.BlockSpec(shape, lambda i: (0,) * len(shape))
    rows = lambda n: pl.BlockSpec((tm, n), lambda i: (i, 0))
    q_t = pl.BlockSpec((1, AUG, tm), lambda i: (i // bps, 0, i % bps))
    v_t = pl.BlockSpec((1, 1, nv, tm), lambda i: (i // bps, i % bps, 0, 0))
    widths = (AUG, 4 * GROUP, AUG, RWKV_PROJ)
    augt_bias = aug_rows.T.reshape(N_HEADS, 128, 16)[:, AUG_LANE:, :].reshape(GROUP, 16)
    q_shape = jax.ShapeDtypeStruct((batch, AUG, seq), BF16)
    v_shape = jax.ShapeDtypeStruct((batch, bps, nv, tm), BF16)
    return pl.pallas_call(
        functools.partial(_in_proj_kernel, tm=tm, tq=tq, blocks_per_seq=bps),
        grid=(t // tm,),
        in_specs=[rows(D_MODEL), full((1, D_MODEL)), full((D_MODEL, N_MAIN)), full((N_T, D_MODEL)),
                  full((D_MODEL, N_SMALL)), full((16, AUG)), full((GROUP, 16)),
                  full((1, N_SMALL)), full((3, N_SMALL, AUG))],
        out_specs=[rows(n) for n in widths] + [rows(N_SMALL), q_t, v_t, q_t, v_t],
        out_shape=[jax.ShapeDtypeStruct((t, n), BF16) for n in widths]
        + [jax.ShapeDtypeStruct((t, N_SMALL), F32), q_shape, v_shape, q_shape, v_shape],
        scratch_shapes=[pltpu.VMEM((8, N_SMALL), F32)],
        compiler_params=_params("arbitrary"),
        name="in_proj",
    )(x2, g, w_main, w_t, w_small, aug_rows, augt_bias, fbias, place)


def _head_mean_matrix():
    same = _div(_iota((GROUP, GROUP), 0), HEAD_DIM) == _div(_iota((GROUP, GROUP), 1), HEAD_DIM)
    return jnp.where(same, 1.0 / HEAD_DIM, 0.0).astype(BF16)


def _head_mean(x):
    return _dot(x.astype(BF16), _head_mean_matrix())


def _attn_kernel(*refs, kind, tq):
    if kind == "diff":
        (qt_ref, k_ref, vt_ref, lq1_ref, lk1_ref, lq2_ref, lk2_ref, g_ref, o_ref,
         qs_ref, m_ref, acc_ref) = refs
        n_c = 2
    else:
        qt_ref, k_ref, vt_ref, g_ref, o_ref, qs_ref, m_ref, acc_ref = refs
        n_c = 1
    tk = tq
    i = pl.program_id(1)

    sub = _iota((128, 1), 0)
    for h in range(N_HEADS):
        qh = qt_ref[0, h * 128:(h + 1) * 128, :]
        if kind == "diff":
            zero = jnp.zeros_like(qh)
            qs_ref[h, :, 0:tq] = jnp.where((sub >= DIFF_QK) & (sub < 2 * DIFF_QK), zero, qh)
            qs_ref[h, :, tq:2 * tq] = jnp.where(sub < DIFF_QK, zero, qh)
        else:
            qs_ref[h] = qh
    m_ref[...] = jnp.full(m_ref.shape, -jnp.inf, F32)
    acc_ref[...] = jnp.zeros(acc_ref.shape, F32)
    wu = min(ATT_UNIT_WIDTH, tq)
    units = [(h, l0) for h in range(N_HEADS) for l0 in range(0, n_c * tq, wu)]
    kpos = _iota((tk, wu), 0)
    qcol = _iota((tk, wu), 1)

    def run(kv_steps):
        work = [(j, masked, h, l0) for j, masked in kv_steps for h, l0 in units]

        def logits(unit):
            j, _, h, l0 = unit
            start = pl.multiple_of(j * tk, tk)
            return _dot(k_ref[0, pl.ds(start, tk), h * 128:(h + 1) * 128], qs_ref[h, :, l0:l0 + wu])

        def stats(unit, s):
            j, masked, h, l0 = unit
            slope2 = LOG2E * 2.0 ** (-(ALIBI_MAX_EXP / N_HEADS) * (h + 1))
            off = -slope2 * ((i - j) * tq).astype(F32) if kind == "diff" else 0.0
            if masked:
                s = jnp.where(kpos <= qcol + (l0 % tq), s, -jnp.inf)
            m_old = m_ref[h, :, l0:l0 + wu]
            m_new = jnp.maximum(m_old, jnp.max(s, axis=0, keepdims=True) + off)
            m_ref[h, :, l0:l0 + wu] = m_new
            return s, jnp.exp2(m_old - m_new), m_new - off

        def accumulate(unit, s, alpha, shift):
            j, _, h, l0 = unit
            p = jnp.exp2(s - shift).astype(BF16)
            v_h = vt_ref[0, j, h * V_ROWS:(h + 1) * V_ROWS, :]
            acc_ref[h, :, l0:l0 + wu] = acc_ref[h, :, l0:l0 + wu] * alpha + _dot(v_h, p)

        n_work = len(work)
        s_of = {0: logits(work[0])}
        if n_work > 1:
            s_of[1] = logits(work[1])
        st_of = {0: stats(work[0], s_of.pop(0))}
        for n in range(n_work):
            if n + 2 < n_work:
                s_of[n + 2] = logits(work[n + 2])
            if n + 1 < n_work:
                st_of[n + 1] = stats(work[n + 1], s_of.pop(n + 1))
            accumulate(work[n], *st_of.pop(n))

    def body(t, carry):
        run([(2 * t, False), (2 * t + 1, False)])
        return carry

    lax.fori_loop(0, i // 2, body, 0)

    @pl.when(i % 2 == 1)
    def _():
        run([(i - 1, False), (i, True)])

    @pl.when(i % 2 == 0)
    def _():
        run([(i, True)])

    parts = []
    if kind == "diff":
        lam_init = g_ref[1:2, 0:1]
        lam = (jnp.exp(jnp.sum(lq1_ref[...] * lk1_ref[...], axis=1, keepdims=True))
               - jnp.exp(jnp.sum(lq2_ref[...] * lk2_ref[...], axis=1, keepdims=True)) + lam_init)
        for h in range(N_HEADS):
            on = acc_ref[h, 0:HEAD_DIM, :] * (1.0 / acc_ref[h, HEAD_DIM:HEAD_DIM + 1, :])
            parts.append(on[:, 0:tq] - lam * on[:, tq:2 * tq])
    else:
        for h in range(N_HEADS):
            parts.append(acc_ref[h, 0:HEAD_DIM, :] * (1.0 / acc_ref[h, HEAD_DIM:HEAD_DIM + 1, :]))
    o = jnp.concatenate(parts, axis=0).T
    ms = _head_mean(o * o)
    y = o * lax.rsqrt(ms + NORM_EPS) * g_ref[0:1, :]
    if kind == "diff":
        y = y * (1.0 - lam_init)
    o_ref[0] = y.astype(o_ref.dtype)


def _attention(kind, q_t, k_aug, v_t, extra, gains, tq):
    b, s, _ = k_aug.shape
    nq = s // tq
    w = (2 if kind == "diff" else 1) * tq
    extra_specs = [pl.BlockSpec((1, DIFF_QK), lambda bi, i: (0, 0))] * len(extra)
    once = pl.Buffered(1)
    return pl.pallas_call(
        functools.partial(_attn_kernel, kind=kind, tq=tq),
        grid=(b, nq),
        in_specs=[pl.BlockSpec((1, AUG, tq), lambda bi, i: (bi, 0, i)),
                  pl.BlockSpec((1, s, AUG), lambda bi, i: (bi, 0, 0), pipeline_mode=once),
                  pl.BlockSpec((1, nq, N_HEADS * V_ROWS, tq), lambda bi, i: (bi, 0, 0, 0), pipeline_mode=once)]
        + extra_specs + [pl.BlockSpec((8, GROUP), lambda bi, i: (0, 0))],
        out_specs=pl.BlockSpec((1, tq, GROUP), lambda bi, i: (bi, i, 0)),
        out_shape=jax.ShapeDtypeStruct((b, s, GROUP), BF16),
        scratch_shapes=[pltpu.VMEM((N_HEADS, 128, w), BF16), pltpu.VMEM((N_HEADS, 1, w), F32),
                        pltpu.VMEM((N_HEADS, V_ROWS, w), F32)],
        compiler_params=_params("arbitrary", "arbitrary"),
        name=kind + "_attention",
    )(q_t, k_aug, v_t, *extra, gains)


def _ssd_kernel(u_ref, sm_ref, cw_ref, cb_ref, dtb_ref, a_ref, d_ref, g_ref, o_ref,
                prev_ref, state_ref):
    L = SSM_CHUNK
    c = pl.program_id(1)

    @pl.when(c == 0)
    def _():
        prev_ref[...] = jnp.zeros(prev_ref.shape, F32)
        state_ref[...] = jnp.zeros(state_ref.shape, F32)

    T = u_ref.shape[1]
    chunks = range(T // L)
    u = u_ref[0].astype(F32)
    z = u[:, 0:GROUP]
    raw = u[:, GROUP:]
    tail = prev_ref[...]
    row8 = _iota((8, 1), 0)
    xbc = raw * cw_ref[SSM_CONV - 1:SSM_CONV, :] + cb_ref[...]
    for d in range(1, SSM_CONV):
        rolled = pltpu.roll(raw, d, 0)
        head = jnp.where(row8 < d, pltpu.roll(tail, d, 0), rolled[0:8])
        shifted = jnp.concatenate([head, rolled[8:]], axis=0)
        xbc = xbc + shifted * cw_ref[SSM_CONV - 1 - d:SSM_CONV - d, :]
    prev_ref[...] = raw[T - 8:T, :]
    xbc = _silu(xbc)
    xs = xbc[:, 0:GROUP]
    bm = xbc[:, GROUP:2 * GROUP]
    cm = xbc[:, 2 * GROUP:3 * GROUP]

    dt = _softplus(sm_ref[0] + dtb_ref[...])
    da = dt * a_ref[...]
    rep = (_iota((128, 4 * 128), 0) == _div(_iota((128, 4 * 128), 1), 128)).astype(BF16)
    exp_h = (_iota((128, GROUP), 0) == _div(_iota((128, GROUP), 1), HEAD_DIM)).astype(BF16)
    dt_e = _dot3_right(dt, exp_h)
    ti = _iota((T, T), 0)
    tj = _iota((T, T), 1)
    tri = ((ti >= tj) & (_div(ti, L) == _div(tj, L))).astype(BF16)
    acs = _dot3_right(_dot3_left(tri, da), rep)
    causal = _iota((L, L), 0) >= _iota((L, L), 1)
    xdt = (xs * dt_e).astype(BF16)
    hl = _div(_iota((1, GROUP), 1), HEAD_DIM)
    blk = _div(_iota((4 * 128, GROUP), 0), 128) == _div(_iota((4 * 128, GROUP), 1), HEAD_DIM)

    last, y_diag, cdc, st_new, chunk_decay = [], [], [], [], []
    for n in chunks:
        rows = slice(n * L, (n + 1) * L)
        last_n = acs[n * L + L - 1:n * L + L, :]
        scores, bd, cd = [], [], []
        for h in range(N_HEADS):
            g = h // 2
            col = acs[rows, h * 128:(h + 1) * 128]
            b_g = bm[rows, g * 128:(g + 1) * 128]
            c_g = cm[rows, g * 128:(g + 1) * 128]
            if h % 2 == 0:
                cb = _dot_nt(c_g.astype(BF16), b_g.astype(BF16))
            decay = jnp.exp(jnp.where(causal, col - col.T, -jnp.inf))
            scores.append((cb * decay).astype(BF16))
            bd.append((b_g * jnp.exp(last_n[:, h * 128:(h + 1) * 128] - col)).astype(BF16))
            cd.append((c_g * jnp.exp(col)).astype(BF16))
        y_st = _dot(jnp.concatenate(scores, axis=0), xdt[rows])
        y_n = jnp.zeros((L, GROUP), F32)
        for h in range(N_HEADS):
            y_n = y_n + jnp.where(hl == h, y_st[h * L:(h + 1) * L, :], 0.0)
        y_diag.append(y_n)
        cdc.append(jnp.concatenate(cd, axis=1))
        st_new.append(jnp.where(blk, _dot_tn(jnp.concatenate(bd, axis=1), xdt[rows]), 0.0))
        chunk_decay.append(jnp.concatenate(
            [jnp.broadcast_to(jnp.exp(last_n[:, h * 128:h * 128 + 1]), (128, GROUP)) for h in range(N_HEADS)],
            axis=0))
    state = state_ref[...]
    ys = []
    for n in chunks:
        ys.append(y_diag[n] + _dot(cdc[n], state.astype(BF16)))
        state = state * chunk_decay[n] + st_new[n]
    state_ref[...] = state
    y = jnp.concatenate(ys, axis=0)

    y = y + xs * d_ref[...]
    yg = y * _silu(z)
    sq = yg * yg
    half = GROUP // 2
    ms0 = jnp.mean(sq[:, 0:half], axis=1, keepdims=True)
    ms1 = jnp.mean(sq[:, half:], axis=1, keepdims=True)
    ms = jnp.where(_iota((1, GROUP), 1) < half, ms0, ms1)
    o_ref[0] = (yg * lax.rsqrt(ms + NORM_EPS) * g_ref[...]).astype(o_ref.dtype)


def _ssd(u, small, conv_w, conv_b, dt_bias, a_neg, d_e, norm_g):
    b, s, _ = u.shape
    L = SSM_CHUNK * SSM_CHUNKS_PER_STEP
    cdim = 3 * GROUP
    full = lambda shape: pl.BlockSpec(shape, lambda bi, c: (0,) * len(shape))
    return pl.pallas_call(
        _ssd_kernel,
        grid=(b, s // L),
        in_specs=[pl.BlockSpec((1, L, 4 * GROUP), lambda bi, c: (bi, c, 0)),
                  pl.BlockSpec((1, L, N_SMALL), lambda bi, c: (bi, c, 0)),
                  full((SSM_CONV, cdim)), full((1, cdim)), full((1, N_SMALL)), full((1, N_SMALL)),
                  full((1, GROUP)), full((1, GROUP))],
        out_specs=pl.BlockSpec((1, L, GROUP), lambda bi, c: (bi, c, 0)),
        out_shape=jax.ShapeDtypeStruct((b, s, GROUP), BF16),
        scratch_shapes=[pltpu.VMEM((8, cdim), F32), pltpu.VMEM((4 * SSM_STATE, GROUP), F32)],
        compiler_params=_params("arbitrary", "arbitrary"),
        name="ssd_mixer",
    )(u, small, conv_w, conv_b, dt_bias, a_neg, d_e, norm_g)


def _rwkv_kernel(p_ref, mu_ref, w0_ref, w2_ref, a0_ref, a2_ref, g2_ref, kk_ref, ka_ref, rk_ref,
                 lnw_ref, lnb_ref, o_ref, last_ref, h_ref):
    C = RWKV_CHUNK
    R = N_HEADS * C
    T = p_ref.shape[1]
    n_chunks = T // C
    c = pl.program_id(1)

    @pl.when(c == 0)
    def _():
        last_ref[...] = jnp.zeros(last_ref.shape, F32)
        h_ref[...] = jnp.zeros(h_ref.shape, F32)

    p = p_ref[0].astype(F32)
    row = _iota((T, 1), 0)
    p_prev = jnp.where(row >= 1, pltpu.roll(p, 1, 0), jnp.broadcast_to(last_ref[0:1, :], p.shape))
    last_ref[0:1, :] = p[T - 1:T, :]
    pm = p + (p_prev - p) * mu_ref[...]
    r = pm[:, 0:GROUP]
    k = pm[:, GROUP:2 * GROUP]
    v = pm[:, 2 * GROUP:3 * GROUP]
    xlow = pm[:, 3 * GROUP:]
    w_raw = -_softplus(-(w0_ref[...] + _dot(jnp.tanh(xlow).astype(BF16), w2_ref[...]))) - 0.5
    logw = -jnp.exp(w_raw)
    a = _sigmoid(a0_ref[...] + _dot(xlow.astype(BF16), a2_ref[...]))
    gate = _dot(_sigmoid(xlow).astype(BF16), g2_ref[...])
    kk = k * kk_ref[...]
    kk = kk * lax.rsqrt(_head_mean(kk * kk) * HEAD_DIM + 1e-12)
    k = k * (1.0 + (a - 1.0) * ka_ref[...])
    bonus = (_head_mean(r * k * rk_ref[...]) * HEAD_DIM) * v

    ti = _iota((T, T), 0)
    tj = _iota((T, T), 1)
    tri = ((ti >= tj) & (_div(ti, C) == _div(tj, C))).astype(BF16)
    lc = _dot3_left(tri, logw)
    e_pos = jnp.exp(lc)
    e_neg = jnp.exp(-lc)
    a_t = -kk * jnp.exp(lc - logw)
    b_t = (a * kk) * e_neg
    k_t = k * e_neg
    r_t = r * e_pos

    hl = _div(_iota((1, GROUP), 1), HEAD_DIM)
    ri = _iota((R, R), 0)
    ci = _iota((R, R), 1)
    same = _div(ri, C) == _div(ci, C)
    strict = same & (_mod(ri, C) > _mod(ci, C))
    incl = same & (_mod(ri, C) >= _mod(ci, C))
    eye = ri == ci
    chunks = range(n_chunks)

    def stack(x, n):
        xc = x[n * C:(n + 1) * C]
        return jnp.concatenate([jnp.where(hl == h, xc, 0.0) for h in range(N_HEADS)], axis=0)

    a_s = [stack(a_t, n) for n in chunks]
    b_s = [stack(b_t, n) for n in chunks]
    k_s = [stack(k_t, n) for n in chunks]
    r_s = [stack(r_t, n) for n in chunks]
    v_sb = [stack(v, n).astype(BF16) for n in chunks]
    pc = [e_pos[n * C + C - 1:n * C + C, :] for n in chunks]
    m_all = [_dot_nt(jnp.concatenate([a_s[n], r_s[n]], axis=0).astype(BF16),
                     jnp.concatenate([b_s[n], k_s[n]], axis=0).astype(BF16)) for n in chunks]
    l_p = [jnp.where(strict, m_all[n][0:R, 0:R], 0.0) for n in chunks]
    m_ak = [jnp.where(strict, m_all[n][0:R, R:], 0.0).astype(BF16) for n in chunks]
    m_rb = [jnp.where(incl, m_all[n][R:, 0:R], 0.0).astype(BF16) for n in chunks]
    m_rk = [jnp.where(incl, m_all[n][R:, R:], 0.0).astype(BF16) for n in chunks]
    t_m = [jnp.where(eye, 1.0, 0.0) + l_p[n] for n in chunks]
    for _ in range(int(math.log2(C)) - 1):
        l_pb = [l_p[n].astype(BF16) for n in chunks]
        l_p = [_dot(l_pb[n], l_pb[n]) for n in chunks]
        t_m = [t_m[n] + _dot(t_m[n].astype(BF16), l_p[n].astype(BF16)) for n in chunks]
    t_b = [t_m[n].astype(BF16) for n in chunks]
    mv = [_dot(m_ak[n], v_sb[n]).astype(BF16) for n in chunks]
    w_sb = [_dot(t_b[n], a_s[n].astype(BF16)).astype(BF16) for n in chunks]
    u0_b = [_dot(t_b[n], mv[n]).astype(BF16) for n in chunks]
    bh = [(b_s[n] * pc[n]).astype(BF16) for n in chunks]
    kh = [(k_s[n] * pc[n]).astype(BF16) for n in chunks]
    g_m = [(jnp.where(eye, jnp.broadcast_to(pc[n], (R, R)), 0.0) + _dot_tn(bh[n], w_sb[n])).astype(BF16)
           for n in chunks]
    d_m = [_dot_tn(bh[n], u0_b[n]) + _dot_tn(kh[n], v_sb[n]) for n in chunks]
    q_m = [(r_s[n] + _dot(m_rb[n], w_sb[n])).astype(BF16) for n in chunks]
    y0 = [_dot(m_rb[n], u0_b[n]) + _dot(m_rk[n], v_sb[n]) for n in chunks]
    state = h_ref[...]
    ys = []
    for n in chunks:
        h_b = state.astype(BF16)
        y_s = _dot(q_m[n], h_b) + y0[n]
        state = _dot(g_m[n], h_b) + d_m[n]
        ys.append(y_s[0:C] + y_s[C:2 * C] + y_s[2 * C:3 * C] + y_s[3 * C:4 * C])
    h_ref[...] = state
    y = jnp.concatenate(ys, axis=0)

    mean = _head_mean(y)
    yc = y - mean
    var = _head_mean(yc * yc)
    o = yc * lax.rsqrt(var + RWKV_LN_EPS) * lnw_ref[...] + lnb_ref[...] + bonus
    o_ref[0] = (o * gate).astype(o_ref.dtype)


def _rwkv(p, mu, w0, w2p, a0, a2p, g2p, k_k, k_a, r_k, ln_w, ln_b):
    b, s, _ = p.shape
    C = RWKV_CHUNK * RWKV_CHUNKS_PER_STEP
    full = lambda shape: pl.BlockSpec(shape, lambda bi, c: (0,) * len(shape))
    vec = full((1, GROUP))
    mat = full((128, GROUP))
    return pl.pallas_call(
        _rwkv_kernel,
        grid=(b, s // C),
        in_specs=[pl.BlockSpec((1, C, RWKV_PROJ), lambda bi, c: (bi, c, 0)), full((1, RWKV_PROJ)),
                  vec, mat, vec, mat, mat, vec, vec, vec, vec, vec],
        out_specs=pl.BlockSpec((1, C, GROUP), lambda bi, c: (bi, c, 0)),
        out_shape=jax.ShapeDtypeStruct((b, s, GROUP), BF16),
        scratch_shapes=[pltpu.VMEM((8, RWKV_PROJ), F32), pltpu.VMEM((GROUP, GROUP), F32)],
        compiler_params=_params("arbitrary", "arbitrary"),
        name="rwkv7_mix",
    )(p, mu, w0, w2p, a0, a2p, g2p, k_k, k_a, r_k, ln_w, ln_b)


def _ffn_kernel(x_ref, y0_ref, y1_ref, y2_ref, y3_ref, wo_ref, go_ref,
                g1_ref, wg_ref, wu_ref, cw_ref, cb_ref, wd_ref, g2_ref, o_ref, tail_ref):
    tm = x_ref.shape[1]
    j = pl.program_id(1)

    @pl.when(j == 0)
    def _():
        tail_ref[...] = jnp.zeros(tail_ref.shape, F32)

    mix = _dot(y0_ref[0], wo_ref[0:GROUP, :])
    mix = mix + _dot(y1_ref[0], wo_ref[GROUP:2 * GROUP, :])
    mix = mix + _dot(y2_ref[0], wo_ref[2 * GROUP:3 * GROUP, :])
    mix = mix + _dot(y3_ref[0], wo_ref[3 * GROUP:4 * GROUP, :])
    ms0 = jnp.mean(mix * mix, axis=-1, keepdims=True)
    x = x_ref[0] + mix * lax.rsqrt(ms0 + NORM_EPS) * go_ref[...]
    ms = jnp.mean(x * x, axis=-1, keepdims=True)
    h = (x * lax.rsqrt(ms + NORM_EPS) * g1_ref[...]).astype(BF16)
    row8 = _iota((8, 1), 0)
    acc = jnp.zeros((tm, D_MODEL), F32)
    for c in range(D_FF // FFN_CHUNK):
        cs = slice(c * FFN_CHUNK, (c + 1) * FFN_CHUNK)
        gate = _dot(h, wg_ref[:, cs])
        up = _dot(h, wu_ref[:, cs])
        tail = tail_ref[:, cs]
        conv = gate * cw_ref[FFN_CONV - 1:FFN_CONV, cs] + cb_ref[:, cs]
        for d in range(1, FFN_CONV):
            rolled = pltpu.roll(gate, d, 0)
            head = jnp.where(row8 < d, pltpu.roll(tail, d, 0), rolled[0:8])
            shifted = jnp.concatenate([head, rolled[8:]], axis=0)
            conv = conv + shifted * cw_ref[FFN_CONV - 1 - d:FFN_CONV - d, cs]
        tail_ref[:, cs] = gate[tm - 8:tm, :]
        inner = 0.7978845608028654 * (conv + 0.044715 * conv * conv * conv)
        act = 0.5 * conv * (1.0 + jnp.tanh(inner))
        acc = acc + _dot((act * up).astype(BF16), wd_ref[cs, :])
    ms2 = jnp.mean(acc * acc, axis=-1, keepdims=True)
    o_ref[0] = x + acc * lax.rsqrt(ms2 + NORM_EPS) * g2_ref[...]


def _out_proj_ffn(x, ys, wo, go, g1, wg, wu, cw, cb, wd, g2, tm=512):
    b, s, _ = x.shape
    tm = min(tm, s)
    full = lambda shape: pl.BlockSpec(shape, lambda bi, j: (0, 0), pipeline_mode=pl.Buffered(1))
    blk = pl.BlockSpec((1, tm, D_MODEL), lambda bi, j: (bi, j, 0))
    yblk = pl.BlockSpec((1, tm, GROUP), lambda bi, j: (bi, j, 0))
    return pl.pallas_call(
        _ffn_kernel,
        grid=(b, s // tm),
        in_specs=[blk] + [yblk] * 4 + [full((D_MODEL, D_MODEL)), full((1, D_MODEL)),
                                       full((1, D_MODEL)), full((D_MODEL, D_FF)), full((D_MODEL, D_FF)),
                                       full((FFN_CONV, D_FF)), full((1, D_FF)), full((D_FF, D_MODEL)),
                                       full((1, D_MODEL))],
        out_specs=blk,
        out_shape=jax.ShapeDtypeStruct((b, s, D_MODEL), F32),
        scratch_shapes=[pltpu.VMEM((8, D_FF), F32)],
        compiler_params=_params("arbitrary", "arbitrary"),
        name="out_proj_conv_glu_ffn",
    )(x, *ys, wo, go, g1, wg, wu, cw, cb, wd, g2)


def _row(v):
    return v.reshape(1, -1).astype(F32)


def _pad_rows(m, start, total=128):
    return jnp.zeros((total, m.shape[1]), m.dtype).at[start:start + m.shape[0]].set(m)


def _bias_lane_constants():
    lanes = jnp.arange(AUG)
    head, local = lanes // 128, lanes % 128
    slope = jnp.exp2(-(ALIBI_MAX_EXP / N_HEADS) * (head + 1).astype(F32))
    at = lambda k: (local == AUG_LANE + k).astype(F32)
    zero = jnp.zeros((AUG,), F32)
    fox_q = -(at(0) + at(1) + at(2))
    c_parts = [p.astype(F32) for p in _split3(slope * LOG2E)]
    c_at = lambda base: sum(c_parts[k] * at(base + k) for k in range(3))
    ones_at = lambda base: at(base) + at(base + 1) + at(base + 2)
    rows = jnp.stack([jnp.full((AUG,), DIFF_QK ** -0.5 * LOG2E, F32), -ones_at(0), -ones_at(3),
                      c_at(6) + c_at(9), ones_at(6), ones_at(9), c_at(0) + c_at(3),
                      jnp.full((AUG,), HEAD_DIM ** -0.5 * LOG2E, F32), fox_q] + [zero] * 7)
    src = jnp.arange(N_SMALL)
    place = jnp.stack([((src[:, None] == 4 + head[None, :]) & (local[None, :] == AUG_LANE + k)).astype(BF16)
                       for k in range(3)])
    return rows, place


def kernel(x, norm_mix_pre, norm_mix_post, norm_ffn_pre, norm_ffn_post, w_in, w_out, diff_lambda_q1, diff_lambda_k1, diff_lambda_q2, diff_lambda_k2, diff_subln, ssm_conv_w, ssm_conv_b, ssm_dt_bias, ssm_a_log, ssm_d, ssm_norm, fox_f_bias, fox_norm, rwkv_mu, rwkv_w0, rwkv_w2, rwkv_a0, rwkv_a2, rwkv_g2, rwkv_k_k, rwkv_k_a, rwkv_r_k, rwkv_ln_w, rwkv_ln_b, ffn_w_gate, ffn_w_up, ffn_conv_w, ffn_conv_b, ffn_w_down):
    b, s, d = x.shape
    t = b * s
    depth = w_in.shape[0]
    tq = min(ATT_BLOCK, s)
    nq = s // tq
    o_dt = 768 + 256 + 768
    o_fq = o_dt + 4
    o_ff = o_fq + 768
    o_rp = o_ff + 4
    aug_rows, place = _bias_lane_constants()
    for l in range(depth):
        wl = w_in[l]
        w_main = jnp.concatenate(
            [wl[:, 256:512], wl[:, 768:o_dt], wl[:, o_fq + 256:o_fq + 512], wl[:, o_rp:]], axis=1).astype(BF16)
        w_t = jnp.concatenate(
            [wl[:, 0:256], wl[:, 512:768], wl[:, o_fq:o_fq + 256], wl[:, o_fq + 512:o_ff]], axis=1).T.astype(BF16)
        w_small = jnp.concatenate([wl[:, o_dt:o_fq], wl[:, o_ff:o_rp],
                                   jnp.zeros((d, N_SMALL - 8), F32)], axis=1).astype(BF16)
        fbias = jnp.zeros((1, N_SMALL), F32).at[0, 4:8].set(fox_f_bias[l])
        (kd, u_ssm, kf, u_rwkv, u_small, qd_t, vd_t, qf_t, vf_t) = _in_proj(
            x.reshape(t, d), _row(norm_mix_pre[l]), w_main, w_t, w_small, aug_rows, fbias, place, b, s, tq)
        u_ssm = u_ssm.reshape(b, s, -1)
        u_rwkv = u_rwkv.reshape(b, s, -1)
        u_small = u_small.reshape(b, s, -1)

        lam_init = 0.8 - 0.6 * math.exp(-0.3 * l)
        g_diff = jnp.zeros((8, GROUP), F32).at[0].set(jnp.tile(diff_subln[l], N_HEADS)).at[1].set(lam_init)
        y_diff = _attention("diff", qd_t, kd.reshape(b, s, AUG), vd_t,
                            [_row(diff_lambda_q1[l]), _row(diff_lambda_k1[l]),
                             _row(diff_lambda_q2[l]), _row(diff_lambda_k2[l])], g_diff, tq)
        g_fox = jnp.zeros((8, GROUP), F32).at[0].set(jnp.tile(fox_norm[l], N_HEADS))
        y_fox = _attention("fox", qf_t, kf.reshape(b, s, AUG), vf_t, [], g_fox, tq)

        pad4 = lambda v: jnp.zeros((1, N_SMALL), F32).at[0, 0:4].set(v)
        y_ssm = _ssd(u_ssm, u_small, ssm_conv_w[l], _row(ssm_conv_b[l]), pad4(ssm_dt_bias[l]),
                     pad4(-jnp.exp(ssm_a_log[l])), _row(jnp.repeat(ssm_d[l], HEAD_DIM)), _row(ssm_norm[l]))

        y_rwkv = _rwkv(u_rwkv, _row(rwkv_mu[l]), _row(rwkv_w0[l]),
                       _pad_rows(rwkv_w2[l], 0).astype(BF16), _row(rwkv_a0[l]),
                       _pad_rows(rwkv_a2[l], 32).astype(BF16), _pad_rows(rwkv_g2[l], 64).astype(BF16),
                       _row(rwkv_k_k[l]), _row(rwkv_k_a[l]), _row(rwkv_r_k[l]),
                       _row(rwkv_ln_w[l]), _row(rwkv_ln_b[l]))

        x = _out_proj_ffn(x, [y_diff, y_ssm, y_fox, y_rwkv], w_out[l].astype(BF16), _row(norm_mix_post[l]),
                          _row(norm_ffn_pre[l]), ffn_w_gate[l].astype(BF16), ffn_w_up[l].astype(BF16),
                          ffn_conv_w[l], _row(ffn_conv_b[l]), ffn_w_down[l].astype(BF16),
                          _row(norm_ffn_post[l]))
    return x
```

```python
import functools
import math

import jax
import jax.numpy as jnp
from jax import lax
from jax.experimental import pallas as pl
from jax.experimental.pallas import tpu as pltpu

F32 = jnp.float32
BF16 = jnp.bfloat16

D_MODEL = 1024
GROUP = 256
N_HEADS = 4
HEAD_DIM = 64
DIFF_QK = 32
NORM_EPS = 1e-6
ALIBI_MAX_EXP = 8.0
SSM_STATE = 128
SSM_CONV = 4
SSM_CHUNK = 128
SSM_CHUNKS_PER_STEP = 2
RWKV_CHUNK = 64
RWKV_CHUNKS_PER_STEP = 4
RWKV_PROJ = 896
RWKV_LN_EPS = 64e-5
D_FF = 2816
FFN_CONV = 3
FFN_CHUNK = 2816
AUG = 2 * GROUP
AUG_LANE = HEAD_DIM
V_ROWS = HEAD_DIM + 16
LOG2E = 1.4426950408889634
N_T = 4 * GROUP
COLS = (0, GROUP, 5 * GROUP, 6 * GROUP, 6 * GROUP + RWKV_PROJ)
N_MAIN = COLS[-1]
N_SMALL = 128
ATT_BLOCK = 512
ATT_KV_UNROLL = {"diff": 2, "fox": 4}
ATT_UNIT_WIDTH = 512
VMEM_LIMIT = 56 * 1024 * 1024


def _dot(a, b):
    return jnp.dot(a, b, preferred_element_type=F32)


def _dot_nt(a, b):
    return lax.dot_general(a, b, (((1,), (1,)), ((), ())), preferred_element_type=F32)


def _dot_tn(a, b):
    return lax.dot_general(a, b, (((0,), (0,)), ((), ())), preferred_element_type=F32)


def _split3(x):
    h1 = x.astype(BF16)
    r1 = x - h1.astype(F32)
    h2 = r1.astype(BF16)
    h3 = (r1 - h2.astype(F32)).astype(BF16)
    return h1, h2, h3


def _dot3_left(m_bf16, x):
    h1, h2, h3 = _split3(x)
    return _dot(m_bf16, h1) + _dot(m_bf16, h2) + _dot(m_bf16, h3)


def _dot3_right(x, m_bf16):
    h1, h2, h3 = _split3(x)
    return _dot(h1, m_bf16) + _dot(h2, m_bf16) + _dot(h3, m_bf16)


def _iota(shape, dim):
    return lax.broadcasted_iota(jnp.int32, shape, dim)


def _div(x, n):
    return lax.shift_right_logical(x, int(math.log2(n)))


def _mod(x, n):
    return lax.bitwise_and(x, n - 1)


def _softplus(x):
    return jnp.maximum(x, 0.0) + jnp.log(1.0 + jnp.exp(-jnp.abs(x)))


def _sigmoid(x):
    return 1.0 / (1.0 + jnp.exp(-x))


def _silu(x):
    return x * _sigmoid(x)


def _params(*sem):
    return pltpu.CompilerParams(dimension_semantics=sem, vmem_limit_bytes=VMEM_LIMIT)


def _in_proj_kernel(x_ref, g_ref, w_ref, wt_ref, ws_ref, aug_ref, augt_ref, fb_ref, place_ref,
                    kd_ref, os_ref, kf_ref, or_ref, osm_ref, qdt_ref, vdt_ref, qft_ref, vft_ref,
                    carry_ref, *, tm, tq, blocks_per_seq):
    i = pl.program_id(0)

    @pl.when(i % blocks_per_seq == 0)
    def _():
        carry_ref[...] = jnp.zeros(carry_ref.shape, F32)

    x = x_ref[...]
    ms = jnp.mean(x * x, axis=-1, keepdims=True)
    h = (x * lax.rsqrt(ms + NORM_EPS) * g_ref[...]).astype(BF16)
    proj = lambda n: _dot(h, w_ref[:, COLS[n]:COLS[n + 1]])
    proj_t = lambda n: _dot_nt(wt_ref[n * GROUP:(n + 1) * GROUP, :], h)
    col = lambda k: augt_ref[:, k:k + 1]
    pos = _mod(_iota((tm, 1), 0), tq)
    lo = _mod(pos, 256).astype(F32)
    hi = pos.astype(F32) - lo
    pos_t = _mod(_iota((1, tm), 1), tq)
    lo_t = _mod(pos_t, 256).astype(F32)
    hi_t = pos_t.astype(F32) - lo_t
    half = HEAD_DIM

    def head_groups_t(data, bias):
        return jnp.concatenate([part[hd * half:(hd + 1) * half] for hd in range(N_HEADS)
                                for part in (data, bias)], axis=0)

    def value_groups_t(data):
        ones = jnp.ones((V_ROWS - HEAD_DIM, tm), F32)
        return jnp.concatenate([part for hd in range(N_HEADS)
                                for part in (data[hd * half:(hd + 1) * half], ones)], axis=0)

    def head_groups(data, bias):
        low = _iota((1, 128), 1) < half
        out = []
        for pair in range(N_HEADS // 2):
            x2 = data[:, pair * 128:(pair + 1) * 128]
            out.append(jnp.where(low, x2, bias[:, 2 * pair * 128:(2 * pair + 1) * 128]))
            out.append(jnp.where(low, pltpu.roll(x2, half, 1), bias[:, (2 * pair + 1) * 128:(2 * pair + 2) * 128]))
        return jnp.concatenate(out, axis=1)

    qd_bias = col(1) * lo_t + col(2) * hi_t + col(3)
    qdt_ref[0] = head_groups_t(proj_t(0) * (DIFF_QK ** -0.5 * LOG2E), qd_bias).astype(BF16)
    vdt_ref[0, 0] = value_groups_t(proj_t(1)).astype(BF16)
    qft_ref[0] = head_groups_t(proj_t(2) * (HEAD_DIM ** -0.5 * LOG2E),
                               jnp.broadcast_to(col(8), (GROUP, tm))).astype(BF16)
    vft_ref[0, 0] = value_groups_t(proj_t(3)).astype(BF16)
    kd_bias = lo * aug_ref[4:5, :] + hi * aug_ref[5:6, :] + aug_ref[6:7, :]
    kd_ref[...] = head_groups(proj(0), kd_bias).astype(BF16)
    os_ref[...] = proj(1).astype(BF16)
    small = _dot(h, ws_ref[...])
    osm_ref[...] = small
    f = small + fb_ref[...]
    ls = jnp.minimum(f, 0.0) - jnp.log(1.0 + jnp.exp(-jnp.abs(f)))
    tri = (_iota((tm, tm), 0) >= _iota((tm, tm), 1)).astype(BF16)
    cum = _dot3_left(tri, ls) + carry_ref[0:1, :]
    carry_ref[0:1, :] = cum[tm - 1:tm, :]
    c1, c2, c3 = _split3(cum * LOG2E)
    kbias = _dot(c1, place_ref[0]) + _dot(c2, place_ref[1]) + _dot(c3, place_ref[2])
    kf_ref[...] = head_groups(proj(2), kbias).astype(BF16)
    or_ref[...] = proj(3).astype(BF16)


def _in_proj(x2, g, w_main, w_t, w_small, aug_rows, fbias, place, batch, seq, tq):
    t = x2.shape[0]
    tm = tq
    bps = seq // tm
    nv = N_HEADS * V_ROWS
    full = lambda shape: pl.BlockSpec(shape, lambda i: (0,) * len(shape))
    rows = lambda n: pl.BlockSpec((tm, n), lambda i: (i, 0))
    q_t = pl.BlockSpec((1, AUG, tm), lambda i: (i // bps, 0, i % bps))
    v_t = pl.BlockSpec((1, 1, nv, tm), lambda i: (i // bps, i % bps, 0, 0))
    widths = (AUG, 4 * GROUP, AUG, RWKV_PROJ)
    augt_bias = aug_rows.T.reshape(N_HEADS, 128, 16)[:, AUG_LANE:, :].reshape(GROUP, 16)
    q_shape = jax.ShapeDtypeStruct((batch, AUG, seq), BF16)
    v_shape = jax.ShapeDtypeStruct((batch, bps, nv, tm), BF16)
    return pl.pallas_call(
        functools.partial(_in_proj_kernel, tm=tm, tq=tq, blocks_per_seq=bps),
        grid=(t // tm,),
        in_specs=[rows(D_MODEL), full((1, D_MODEL)), full((D_MODEL, N_MAIN)), full((N_T, D_MODEL)),
                  full((D_MODEL, N_SMALL)), full((16, AUG)), full((GROUP, 16)),
                  full((1, N_SMALL)), full((3, N_SMALL, AUG))],
        out_specs=[rows(n) for n in widths] + [rows(N_SMALL), q_t, v_t, q_t, v_t],
        out_shape=[jax.ShapeDtypeStruct((t, n), BF16) for n in widths]
        + [jax.ShapeDtypeStruct((t, N_SMALL), F32), q_shape, v_shape, q_shape, v_shape],
        scratch_shapes=[pltpu.VMEM((8, N_SMALL), F32)],
        compiler_params=_params("arbitrary"),
        name="in_proj",
    )(x2, g, w_main, w_t, w_small, aug_rows, augt_bias, fbias, place)


def _head_mean_matrix():
    same = _div(_iota((GROUP, GROUP), 0), HEAD_DIM) == _div(_iota((GROUP, GROUP), 1), HEAD_DIM)
    return jnp.where(same, 1.0 / HEAD_DIM, 0.0).astype(BF16)


def _head_mean(x):
    return _dot(x.astype(BF16), _head_mean_matrix())


def _attn_kernel(*refs, kind, tq):
    if kind == "diff":
        (qt_ref, k_ref, vt_ref, lq1_ref, lk1_ref, lq2_ref, lk2_ref, g_ref, o_ref,
         qs_ref, m_ref, acc_ref) = refs
        n_c = 2
    else:
        qt_ref, k_ref, vt_ref, g_ref, o_ref, qs_ref, m_ref, acc_ref = refs
        n_c = 1
    tk = tq
    i = pl.program_id(1)

    sub = _iota((128, 1), 0)
    for h in range(N_HEADS):
        qh = qt_ref[0, h * 128:(h + 1) * 128, :]
        if kind == "diff":
            zero = jnp.zeros_like(qh)
            qs_ref[h, :, 0:tq] = jnp.where((sub >= DIFF_QK) & (sub < 2 * DIFF_QK), zero, qh)
            qs_ref[h, :, tq:2 * tq] = jnp.where(sub < DIFF_QK, zero, qh)
        else:
            qs_ref[h] = qh
    m_ref[...] = jnp.full(m_ref.shape, -jnp.inf, F32)
    acc_ref[...] = jnp.zeros(acc_ref.shape, F32)
    wu = min(ATT_UNIT_WIDTH, tq)
    units = [(h, l0) for h in range(N_HEADS) for l0 in range(0, n_c * tq, wu)]
    kpos = _iota((tk, wu), 0)
    qcol = _iota((tk, wu), 1)

    def run(kv_steps):
        work = [(j, masked, h, l0) for j, masked in kv_steps for h, l0 in units]

        def logits(unit):
            j, _, h, l0 = unit
            start = pl.multiple_of(j * tk, tk)
            return _dot(k_ref[0, pl.ds(start, tk), h * 128:(h + 1) * 128], qs_ref[h, :, l0:l0 + wu])

        def stats(unit, s):
            j, masked, h, l0 = unit
            slope2 = LOG2E * 2.0 ** (-(ALIBI_MAX_EXP / N_HEADS) * (h + 1))
            off = -slope2 * ((i - j) * tq).astype(F32) if kind == "diff" else 0.0
            if masked:
                s = jnp.where(kpos <= qcol + (l0 % tq), s, -jnp.inf)
            m_old = m_ref[h, :, l0:l0 + wu]
            m_new = jnp.maximum(m_old, jnp.max(s, axis=0, keepdims=True) + off)
            m_ref[h, :, l0:l0 + wu] = m_new
            return s, jnp.exp2(m_old - m_new), m_new - off

        def accumulate(unit, s, alpha, shift):
            j, _, h, l0 = unit
            p = jnp.exp2(s - shift).astype(BF16)
            v_h = vt_ref[0, j, h * V_ROWS:(h + 1) * V_ROWS, :]
            acc_ref[h, :, l0:l0 + wu] = acc_ref[h, :, l0:l0 + wu] * alpha + _dot(v_h, p)

        n_work = len(work)
        s_of = {0: logits(work[0])}
        if n_work > 1:
            s_of[1] = logits(work[1])
        st_of = {0: stats(work[0], s_of.pop(0))}
        for n in range(n_work):
            if n + 2 < n_work:
                s_of[n + 2] = logits(work[n + 2])
            if n + 1 < n_work:
                st_of[n + 1] = stats(work[n + 1], s_of.pop(n + 1))
            accumulate(work[n], *st_of.pop(n))

    unroll = ATT_KV_UNROLL[kind]

    def body(t, carry):
        run([(unroll * t + u, False) for u in range(unroll)])
        return carry

    lax.fori_loop(0, i // unroll, body, 0)

    for rem in range(unroll):
        @pl.when(i % unroll == rem)
        def _():
            run([(i - rem + u, False) for u in range(rem)] + [(i, True)])

    parts = []
    if kind == "diff":
        lam_init = g_ref[1:2, 0:1]
        lam = (jnp.exp(jnp.sum(lq1_ref[...] * lk1_ref[...], axis=1, keepdims=True))
               - jnp.exp(jnp.sum(lq2_ref[...] * lk2_ref[...], axis=1, keepdims=True)) + lam_init)
        for h in range(N_HEADS):
            on = acc_ref[h, 0:HEAD_DIM, :] * (1.0 / acc_ref[h, HEAD_DIM:HEAD_DIM + 1, :])
            parts.append(on[:, 0:tq] - lam * on[:, tq:2 * tq])
    else:
        for h in range(N_HEADS):
            parts.append(acc_ref[h, 0:HEAD_DIM, :] * (1.0 / acc_ref[h, HEAD_DIM:HEAD_DIM + 1, :]))
    o = jnp.concatenate(parts, axis=0).T
    ms = _head_mean(o * o)
    y = o * lax.rsqrt(ms + NORM_EPS) * g_ref[0:1, :]
    if kind == "diff":
        y = y * (1.0 - lam_init)
    o_ref[0] = y.astype(o_ref.dtype)


def _attention(kind, q_t, k_aug, v_t, extra, gains, tq):
    b, s, _ = k_aug.shape
    nq = s // tq
    w = (2 if kind == "diff" else 1) * tq
    extra_specs = [pl.BlockSpec((1, DIFF_QK), lambda bi, i: (0, 0))] * len(extra)
    once = pl.Buffered(1)
    return pl.pallas_call(
        functools.partial(_attn_kernel, kind=kind, tq=tq),
        grid=(b, nq),
        in_specs=[pl.BlockSpec((1, AUG, tq), lambda bi, i: (bi, 0, i)),
                  pl.BlockSpec((1, s, AUG), lambda bi, i: (bi, 0, 0), pipeline_mode=once),
                  pl.BlockSpec((1, nq, N_HEADS * V_ROWS, tq), lambda bi, i: (bi, 0, 0, 0), pipeline_mode=once)]
        + extra_specs + [pl.BlockSpec((8, GROUP), lambda bi, i: (0, 0))],
        out_specs=pl.BlockSpec((1, tq, GROUP), lambda bi, i: (bi, i, 0)),
        out_shape=jax.ShapeDtypeStruct((b, s, GROUP), BF16),
        scratch_shapes=[pltpu.VMEM((N_HEADS, 128, w), BF16), pltpu.VMEM((N_HEADS, 1, w), F32),
                        pltpu.VMEM((N_HEADS, V_ROWS, w), F32)],
        compiler_params=_params("arbitrary", "arbitrary"),
        name=kind + "_attention",
    )(q_t, k_aug, v_t, *extra, gains)


def _ssd_kernel(u_ref, sm_ref, cw_ref, cb_ref, dtb_ref, a_ref, d_ref, g_ref, o_ref,
                prev_ref, state_ref):
    L = SSM_CHUNK
    c = pl.program_id(1)

    @pl.when(c == 0)
    def _():
        prev_ref[...] = jnp.zeros(prev_ref.shape, F32)
        state_ref[...] = jnp.zeros(state_ref.shape, F32)

    T = u_ref.shape[1]
    chunks = range(T // L)
    u = u_ref[0].astype(F32)
    z = u[:, 0:GROUP]
    raw = u[:, GROUP:]
    tail = prev_ref[...]
    row8 = _iota((8, 1), 0)
    xbc = raw * cw_ref[SSM_CONV - 1:SSM_CONV, :] + cb_ref[...]
    for d in range(1, SSM_CONV):
        rolled = pltpu.roll(raw, d, 0)
        head = jnp.where(row8 < d, pltpu.roll(tail, d, 0), rolled[0:8])
        shifted = jnp.concatenate([head, rolled[8:]], axis=0)
        xbc = xbc + shifted * cw_ref[SSM_CONV - 1 - d:SSM_CONV - d, :]
    prev_ref[...] = raw[T - 8:T, :]
    xbc = _silu(xbc)
    xs = xbc[:, 0:GROUP]
    bm = xbc[:, GROUP:2 * GROUP]
    cm = xbc[:, 2 * GROUP:3 * GROUP]

    dt = _softplus(sm_ref[0] + dtb_ref[...])
    da = dt * a_ref[...]
    rep = (_iota((128, 4 * 128), 0) == _div(_iota((128, 4 * 128), 1), 128)).astype(BF16)
    exp_h = (_iota((128, GROUP), 0) == _div(_iota((128, GROUP), 1), HEAD_DIM)).astype(BF16)
    dt_e = _dot3_right(dt, exp_h)
    ti = _iota((T, T), 0)
    tj = _iota((T, T), 1)
    tri = ((ti >= tj) & (_div(ti, L) == _div(tj, L))).astype(BF16)
    acs = _dot3_right(_dot3_left(tri, da), rep)
    causal = _iota((L, L), 0) >= _iota((L, L), 1)
    xdt = (xs * dt_e).astype(BF16)
    hl = _div(_iota((1, GROUP), 1), HEAD_DIM)
    blk = _div(_iota((4 * 128, GROUP), 0), 128) == _div(_iota((4 * 128, GROUP), 1), HEAD_DIM)

    last, y_diag, cdc, st_new, chunk_decay = [], [], [], [], []
    for n in chunks:
        rows = slice(n * L, (n + 1) * L)
        last_n = acs[n * L + L - 1:n * L + L, :]
        scores, bd, cd = [], [], []
        for h in range(N_HEADS):
            g = h // 2
            col = acs[rows, h * 128:(h + 1) * 128]
            b_g = bm[rows, g * 128:(g + 1) * 128]
            c_g = cm[rows, g * 128:(g + 1) * 128]
            if h % 2 == 0:
                cb = _dot_nt(c_g.astype(BF16), b_g.astype(BF16))
            decay = jnp.exp(jnp.where(causal, col - col.T, -jnp.inf))
            scores.append((cb * decay).astype(BF16))
            bd.append((b_g * jnp.exp(last_n[:, h * 128:(h + 1) * 128] - col)).astype(BF16))
            cd.append((c_g * jnp.exp(col)).astype(BF16))
        y_st = _dot(jnp.concatenate(scores, axis=0), xdt[rows])
        y_n = jnp.zeros((L, GROUP), F32)
        for h in range(N_HEADS):
            y_n = y_n + jnp.where(hl == h, y_st[h * L:(h + 1) * L, :], 0.0)
        y_diag.append(y_n)
        cdc.append(jnp.concatenate(cd, axis=1))
        st_new.append(jnp.where(blk, _dot_tn(jnp.concatenate(bd, axis=1), xdt[rows]), 0.0))
        chunk_decay.append(jnp.concatenate(
            [jnp.broadcast_to(jnp.exp(last_n[:, h * 128:h * 128 + 1]), (128, GROUP)) for h in range(N_HEADS)],
            axis=0))
    state = state_ref[...]
    ys = []
    for n in chunks:
        ys.append(y_diag[n] + _dot(cdc[n], state.astype(BF16)))
        state = state * chunk_decay[n] + st_new[n]
    state_ref[...] = state
    y = jnp.concatenate(ys, axis=0)

    y = y + xs * d_ref[...]
    yg = y * _silu(z)
    sq = yg * yg
    half = GROUP // 2
    ms0 = jnp.mean(sq[:, 0:half], axis=1, keepdims=True)
    ms1 = jnp.mean(sq[:, half:], axis=1, keepdims=True)
    ms = jnp.where(_iota((1, GROUP), 1) < half, ms0, ms1)
    o_ref[0] = (yg * lax.rsqrt(ms + NORM_EPS) * g_ref[...]).astype(o_ref.dtype)


def _ssd(u, small, conv_w, conv_b, dt_bias, a_neg, d_e, norm_g):
    b, s, _ = u.shape
    L = SSM_CHUNK * SSM_CHUNKS_PER_STEP
    cdim = 3 * GROUP
    full = lambda shape: pl.BlockSpec(shape, lambda bi, c: (0,) * len(shape))
    return pl.pallas_call(
        _ssd_kernel,
        grid=(b, s // L),
        in_specs=[pl.BlockSpec((1, L, 4 * GROUP), lambda bi, c: (bi, c, 0)),
                  pl.BlockSpec((1, L, N_SMALL), lambda bi, c: (bi, c, 0)),
                  full((SSM_CONV, cdim)), full((1, cdim)), full((1, N_SMALL)), full((1, N_SMALL)),
                  full((1, GROUP)), full((1, GROUP))],
        out_specs=pl.BlockSpec((1, L, GROUP), lambda bi, c: (bi, c, 0)),
        out_shape=jax.ShapeDtypeStruct((b, s, GROUP), BF16),
        scratch_shapes=[pltpu.VMEM((8, cdim), F32), pltpu.VMEM((4 * SSM_STATE, GROUP), F32)],
        compiler_params=_params("arbitrary", "arbitrary"),
        name="ssd_mixer",
    )(u, small, conv_w, conv_b, dt_bias, a_neg, d_e, norm_g)


def _rwkv_kernel(p_ref, mu_ref, w0_ref, w2_ref, a0_ref, a2_ref, g2_ref, kk_ref, ka_ref, rk_ref,
                 lnw_ref, lnb_ref, o_ref, last_ref, h_ref):
    C = RWKV_CHUNK
    R = N_HEADS * C
    T = p_ref.shape[1]
    n_chunks = T // C
    c = pl.program_id(1)

    @pl.when(c == 0)
    def _():
        last_ref[...] = jnp.zeros(last_ref.shape, F32)
        h_ref[...] = jnp.zeros(h_ref.shape, F32)

    p = p_ref[0].astype(F32)
    row = _iota((T, 1), 0)
    p_prev = jnp.where(row >= 1, pltpu.roll(p, 1, 0), jnp.broadcast_to(last_ref[0:1, :], p.shape))
    last_ref[0:1, :] = p[T - 1:T, :]
    pm = p + (p_prev - p) * mu_ref[...]
    r = pm[:, 0:GROUP]
    k = pm[:, GROUP:2 * GROUP]
    v = pm[:, 2 * GROUP:3 * GROUP]
    xlow = pm[:, 3 * GROUP:]
    w_raw = -_softplus(-(w0_ref[...] + _dot(jnp.tanh(xlow).astype(BF16), w2_ref[...]))) - 0.5
    logw = -jnp.exp(w_raw)
    a = _sigmoid(a0_ref[...] + _dot(xlow.astype(BF16), a2_ref[...]))
    gate = _dot(_sigmoid(xlow).astype(BF16), g2_ref[...])
    kk = k * kk_ref[...]
    kk = kk * lax.rsqrt(_head_mean(kk * kk) * HEAD_DIM + 1e-12)
    k = k * (1.0 + (a - 1.0) * ka_ref[...])
    bonus = (_head_mean(r * k * rk_ref[...]) * HEAD_DIM) * v

    ti = _iota((T, T), 0)
    tj = _iota((T, T), 1)
    tri = ((ti >= tj) & (_div(ti, C) == _div(tj, C))).astype(BF16)
    lc = _dot3_left(tri, logw)
    e_pos = jnp.exp(lc)
    e_neg = jnp.exp(-lc)
    a_t = -kk * jnp.exp(lc - logw)
    b_t = (a * kk) * e_neg
    k_t = k * e_neg
    r_t = r * e_pos

    hl = _div(_iota((1, GROUP), 1), HEAD_DIM)
    ri = _iota((R, R), 0)
    ci = _iota((R, R), 1)
    same = _div(ri, C) == _div(ci, C)
    strict = same & (_mod(ri, C) > _mod(ci, C))
    incl = same & (_mod(ri, C) >= _mod(ci, C))
    eye = ri == ci
    chunks = range(n_chunks)

    def stack(x, n):
        xc = x[n * C:(n + 1) * C]
        return jnp.concatenate([jnp.where(hl == h, xc, 0.0) for h in range(N_HEADS)], axis=0)

    a_s = [stack(a_t, n) for n in chunks]
    b_s = [stack(b_t, n) for n in chunks]
    k_s = [stack(k_t, n) for n in chunks]
    r_s = [stack(r_t, n) for n in chunks]
    v_sb = [stack(v, n).astype(BF16) for n in chunks]
    pc = [e_pos[n * C + C - 1:n * C + C, :] for n in chunks]
    m_all = [_dot_nt(jnp.concatenate([a_s[n], r_s[n]], axis=0).astype(BF16),
                     jnp.concatenate([b_s[n], k_s[n]], axis=0).astype(BF16)) for n in chunks]
    l_p = [jnp.where(strict, m_all[n][0:R, 0:R], 0.0) for n in chunks]
    m_ak = [jnp.where(strict, m_all[n][0:R, R:], 0.0).astype(BF16) for n in chunks]
    m_rb = [jnp.where(incl, m_all[n][R:, 0:R], 0.0).astype(BF16) for n in chunks]
    m_rk = [jnp.where(incl, m_all[n][R:, R:], 0.0).astype(BF16) for n in chunks]
    t_m = [jnp.where(eye, 1.0, 0.0) + l_p[n] for n in chunks]
    for _ in range(int(math.log2(C)) - 1):
        l_pb = [l_p[n].astype(BF16) for n in chunks]
        l_p = [_dot(l_pb[n], l_pb[n]) for n in chunks]
        t_m = [t_m[n] + _dot(t_m[n].astype(BF16), l_p[n].astype(BF16)) for n in chunks]
    t_b = [t_m[n].astype(BF16) for n in chunks]
    mv = [_dot(m_ak[n], v_sb[n]).astype(BF16) for n in chunks]
    w_sb = [_dot(t_b[n], a_s[n].astype(BF16)).astype(BF16) for n in chunks]
    u0_b = [_dot(t_b[n], mv[n]).astype(BF16) for n in chunks]
    bh = [(b_s[n] * pc[n]).astype(BF16) for n in chunks]
    kh = [(k_s[n] * pc[n]).astype(BF16) for n in chunks]
    g_m = [(jnp.where(eye, jnp.broadcast_to(pc[n], (R, R)), 0.0) + _dot_tn(bh[n], w_sb[n])).astype(BF16)
           for n in chunks]
    d_m = [_dot_tn(bh[n], u0_b[n]) + _dot_tn(kh[n], v_sb[n]) for n in chunks]
    q_m = [(r_s[n] + _dot(m_rb[n], w_sb[n])).astype(BF16) for n in chunks]
    y0 = [_dot(m_rb[n], u0_b[n]) + _dot(m_rk[n], v_sb[n]) for n in chunks]
    state = h_ref[...]
    ys = []
    for n in chunks:
        h_b = state.astype(BF16)
        y_s = _dot(q_m[n], h_b) + y0[n]
        state = _dot(g_m[n], h_b) + d_m[n]
        ys.append(y_s[0:C] + y_s[C:2 * C] + y_s[2 * C:3 * C] + y_s[3 * C:4 * C])
    h_ref[...] = state
    y = jnp.concatenate(ys, axis=0)

    mean = _head_mean(y)
    yc = y - mean
    var = _head_mean(yc * yc)
    o = yc * lax.rsqrt(var + RWKV_LN_EPS) * lnw_ref[...] + lnb_ref[...] + bonus
    o_ref[0] = (o * gate).astype(o_ref.dtype)


def _rwkv(p, mu, w0, w2p, a0, a2p, g2p, k_k, k_a, r_k, ln_w, ln_b):
    b, s, _ = p.shape
    C = RWKV_CHUNK * RWKV_CHUNKS_PER_STEP
    full = lambda shape: pl.BlockSpec(shape, lambda bi, c: (0,) * len(shape))
    vec = full((1, GROUP))
    mat = full((128, GROUP))
    return pl.pallas_call(
        _rwkv_kernel,
        grid=(b, s // C),
        in_specs=[pl.BlockSpec((1, C, RWKV_PROJ), lambda bi, c: (bi, c, 0)), full((1, RWKV_PROJ)),
                  vec, mat, vec, mat, mat, vec, vec, vec, vec, vec],
        out_specs=pl.BlockSpec((1, C, GROUP), lambda bi, c: (bi, c, 0)),
        out_shape=jax.ShapeDtypeStruct((b, s, GROUP), BF16),
        scratch_shapes=[pltpu.VMEM((8, RWKV_PROJ), F32), pltpu.VMEM((GROUP, GROUP), F32)],
        compiler_params=_params("arbitrary", "arbitrary"),
        name="rwkv7_mix",
    )(p, mu, w0, w2p, a0, a2p, g2p, k_k, k_a, r_k, ln_w, ln_b)


def _ffn_kernel(x_ref, y0_ref, y1_ref, y2_ref, y3_ref, wo_ref, go_ref,
                g1_ref, wg_ref, wu_ref, cw_ref, cb_ref, wd_ref, g2_ref, o_ref, tail_ref):
    tm = x_ref.shape[1]
    j = pl.program_id(1)

    @pl.when(j == 0)
    def _():
        tail_ref[...] = jnp.zeros(tail_ref.shape, F32)

    mix = _dot(y0_ref[0], wo_ref[0:GROUP, :])
    mix = mix + _dot(y1_ref[0], wo_ref[GROUP:2 * GROUP, :])
    mix = mix + _dot(y2_ref[0], wo_ref[2 * GROUP:3 * GROUP, :])
    mix = mix + _dot(y3_ref[0], wo_ref[3 * GROUP:4 * GROUP, :])
    ms0 = jnp.mean(mix * mix, axis=-1, keepdims=True)
    x = x_ref[0] + mix * lax.rsqrt(ms0 + NORM_EPS) * go_ref[...]
    ms = jnp.mean(x * x, axis=-1, keepdims=True)
    h = (x * lax.rsqrt(ms + NORM_EPS) * g1_ref[...]).astype(BF16)
    row8 = _iota((8, 1), 0)
    acc = jnp.zeros((tm, D_MODEL), F32)
    for c in range(D_FF // FFN_CHUNK):
        cs = slice(c * FFN_CHUNK, (c + 1) * FFN_CHUNK)
        gate = _dot(h, wg_ref[:, cs])
        up = _dot(h, wu_ref[:, cs])
        tail = tail_ref[:, cs]
        conv = gate * cw_ref[FFN_CONV - 1:FFN_CONV, cs] + cb_ref[:, cs]
        for d in range(1, FFN_CONV):
            rolled = pltpu.roll(gate, d, 0)
            head = jnp.where(row8 < d, pltpu.roll(tail, d, 0), rolled[0:8])
            shifted = jnp.concatenate([head, rolled[8:]], axis=0)
            conv = conv + shifted * cw_ref[FFN_CONV - 1 - d:FFN_CONV - d, cs]
        tail_ref[:, cs] = gate[tm - 8:tm, :]
        inner = 0.7978845608028654 * (conv + 0.044715 * conv * conv * conv)
        act = 0.5 * conv * (1.0 + jnp.tanh(inner))
        acc = acc + _dot((act * up).astype(BF16), wd_ref[cs, :])
    ms2 = jnp.mean(acc * acc, axis=-1, keepdims=True)
    o_ref[0] = x + acc * lax.rsqrt(ms2 + NORM_EPS) * g2_ref[...]


def _out_proj_ffn(x, ys, wo, go, g1, wg, wu, cw, cb, wd, g2, tm=512):
    b, s, _ = x.shape
    tm = min(tm, s)
    full = lambda shape: pl.BlockSpec(shape, lambda bi, j: (0, 0), pipeline_mode=pl.Buffered(1))
    blk = pl.BlockSpec((1, tm, D_MODEL), lambda bi, j: (bi, j, 0))
    yblk = pl.BlockSpec((1, tm, GROUP), lambda bi, j: (bi, j, 0))
    return pl.pallas_call(
        _ffn_kernel,
        grid=(b, s // tm),
        in_specs=[blk] + [yblk] * 4 + [full((D_MODEL, D_MODEL)), full((1, D_MODEL)),
                                       full((1, D_MODEL)), full((D_MODEL, D_FF)), full((D_MODEL, D_FF)),
                                       full((FFN_CONV, D_FF)), full((1, D_FF)), full((D_FF, D_MODEL)),
                                       full((1, D_MODEL))],
        out_specs=blk,
        out_shape=jax.ShapeDtypeStruct((b, s, D_MODEL), F32),
        scratch_shapes=[pltpu.VMEM((8, D_FF), F32)],
        compiler_params=_params("arbitrary", "arbitrary"),
        name="out_proj_conv_glu_ffn",
    )(x, *ys, wo, go, g1, wg, wu, cw, cb, wd, g2)


def _row(v):
    return v.reshape(1, -1).astype(F32)


def _pad_rows(m, start, total=128):
    return jnp.zeros((total, m.shape[1]), m.dtype).at[start:start + m.shape[0]].set(m)


def _bias_lane_constants():
    lanes = jnp.arange(AUG)
    head, local = lanes // 128, lanes % 128
    slope = jnp.exp2(-(ALIBI_MAX_EXP / N_HEADS) * (head + 1).astype(F32))
    at = lambda k: (local == AUG_LANE + k).astype(F32)
    zero = jnp.zeros((AUG,), F32)
    fox_q = -(at(0) + at(1) + at(2))
    c_parts = [p.astype(F32) for p in _split3(slope * LOG2E)]
    c_at = lambda base: sum(c_parts[k] * at(base + k) for k in range(3))
    ones_at = lambda base: at(base) + at(base + 1) + at(base + 2)
    rows = jnp.stack([jnp.full((AUG,), DIFF_QK ** -0.5 * LOG2E, F32), -ones_at(0), -ones_at(3),
                      c_at(6) + c_at(9), ones_at(6), ones_at(9), c_at(0) + c_at(3),
                      jnp.full((AUG,), HEAD_DIM ** -0.5 * LOG2E, F32), fox_q] + [zero] * 7)
    src = jnp.arange(N_SMALL)
    place = jnp.stack([((src[:, None] == 4 + head[None, :]) & (local[None, :] == AUG_LANE + k)).astype(BF16)
                       for k in range(3)])
    return rows, place


def kernel(x, norm_mix_pre, norm_mix_post, norm_ffn_pre, norm_ffn_post, w_in, w_out, diff_lambda_q1, diff_lambda_k1, diff_lambda_q2, diff_lambda_k2, diff_subln, ssm_conv_w, ssm_conv_b, ssm_dt_bias, ssm_a_log, ssm_d, ssm_norm, fox_f_bias, fox_norm, rwkv_mu, rwkv_w0, rwkv_w2, rwkv_a0, rwkv_a2, rwkv_g2, rwkv_k_k, rwkv_k_a, rwkv_r_k, rwkv_ln_w, rwkv_ln_b, ffn_w_gate, ffn_w_up, ffn_conv_w, ffn_conv_b, ffn_w_down):
    b, s, d = x.shape
    t = b * s
    depth = w_in.shape[0]
    tq = min(ATT_BLOCK, s)
    nq = s // tq
    o_dt = 768 + 256 + 768
    o_fq = o_dt + 4
    o_ff = o_fq + 768
    o_rp = o_ff + 4
    aug_rows, place = _bias_lane_constants()
    for l in range(depth):
        wl = w_in[l]
        w_main = jnp.concatenate(
            [wl[:, 256:512], wl[:, 768:o_dt], wl[:, o_fq + 256:o_fq + 512], wl[:, o_rp:]], axis=1).astype(BF16)
        w_t = jnp.concatenate(
            [wl[:, 0:256], wl[:, 512:768], wl[:, o_fq:o_fq + 256], wl[:, o_fq + 512:o_ff]], axis=1).T.astype(BF16)
        w_small = jnp.concatenate([wl[:, o_dt:o_fq], wl[:, o_ff:o_rp],
                                   jnp.zeros((d, N_SMALL - 8), F32)], axis=1).astype(BF16)
        fbias = jnp.zeros((1, N_SMALL), F32).at[0, 4:8].set(fox_f_bias[l])
        (kd, u_ssm, kf, u_rwkv, u_small, qd_t, vd_t, qf_t, vf_t) = _in_proj(
            x.reshape(t, d), _row(norm_mix_pre[l]), w_main, w_t, w_small, aug_rows, fbias, place, b, s, tq)
        u_ssm = u_ssm.reshape(b, s, -1)
        u_rwkv = u_rwkv.reshape(b, s, -1)
        u_small = u_small.reshape(b, s, -1)

        lam_init = 0.8 - 0.6 * math.exp(-0.3 * l)
        g_diff = jnp.zeros((8, GROUP), F32).at[0].set(jnp.tile(diff_subln[l], N_HEADS)).at[1].set(lam_init)
        y_diff = _attention("diff", qd_t, kd.reshape(b, s, AUG), vd_t,
                            [_row(diff_lambda_q1[l]), _row(diff_lambda_k1[l]),
                             _row(diff_lambda_q2[l]), _row(diff_lambda_k2[l])], g_diff, tq)
        g_fox = jnp.zeros((8, GROUP), F32).at[0].set(jnp.tile(fox_norm[l], N_HEADS))
        y_fox = _attention("fox", qf_t, kf.reshape(b, s, AUG), vf_t, [], g_fox, tq)

        pad4 = lambda v: jnp.zeros((1, N_SMALL), F32).at[0, 0:4].set(v)
        y_ssm = _ssd(u_ssm, u_small, ssm_conv_w[l], _row(ssm_conv_b[l]), pad4(ssm_dt_bias[l]),
                     pad4(-jnp.exp(ssm_a_log[l])), _row(jnp.repeat(ssm_d[l], HEAD_DIM)), _row(ssm_norm[l]))

        y_rwkv = _rwkv(u_rwkv, _row(rwkv_mu[l]), _row(rwkv_w0[l]),
                       _pad_rows(rwkv_w2[l], 0).astype(BF16), _row(rwkv_a0[l]),
                       _pad_rows(rwkv_a2[l], 32).astype(BF16), _pad_rows(rwkv_g2[l], 64).astype(BF16),
                       _row(rwkv_k_k[l]), _row(rwkv_k_a[l]), _row(rwkv_r_k[l]),
                       _row(rwkv_ln_w[l]), _row(rwkv_ln_b[l]))

        x = _out_proj_ffn(x, [y_diff, y_ssm, y_fox, y_rwkv], w_out[l].astype(BF16), _row(norm_mix_post[l]),
                          _row(norm_ffn_pre[l]), ffn_w_gate[l].astype(BF16), ffn_w_up[l].astype(BF16),
                          ffn_conv_w[l], _row(ffn_conv_b[l]), ffn_w_down[l].astype(BF16),
                          _row(norm_ffn_post[l]))
    return x
```
